```python
import jax, jax.numpy as jnp
from jax import lax
import numpy as np

D_MODEL = 4096
BATCH = 32
SEQ = 256
DEPTH = 1
DEC_BATCH = 2
DEC_SEQ = 1024
PAST_LEN = 256

GRID_W = 64
N_HEADS = 16
QK_NOPE_DIM = 128
QK_ROPE_DIM = 64
V_HEAD_DIM = 128
Q_LORA_RANK = 1024
KV_LORA_RANK = 512
MLA_WIDTH = N_HEADS * V_HEAD_DIM
POOL_WIDTH = D_MODEL - MLA_WIDTH
POOL_WINDOWS = (2, 4, 8, 16)
N_POOL_GROUPS = len(POOL_WINDOWS)
POOL_GROUP_DIM = POOL_WIDTH // N_POOL_GROUPS
IN_PROJ_DIM = Q_LORA_RANK + KV_LORA_RANK + QK_ROPE_DIM + POOL_WIDTH
D_FF = 11008
ROPE_THETA = 10000.0
LN_EPS = 1e-5
RMS_EPS = 1e-6
Q_BLOCK = 128
N_MOD = 9
DEEPNORM_ALPHA = (2.0 * DEPTH) ** 0.25
DEEPNORM_BETA = (8.0 * DEPTH) ** -0.25
ATTN_SCALE = (QK_NOPE_DIM + QK_ROPE_DIM) ** -0.5

kernel_name = "hybrid_mla_pool_diffusion_step"


def _plain_ln(x):
    xf = x.astype(jnp.float32)
    mu = jnp.mean(xf, -1, keepdims=True)
    var = jnp.mean(jnp.square(xf - mu), -1, keepdims=True)
    return ((xf - mu) * lax.rsqrt(var + LN_EPS)).astype(x.dtype)


def _affine_ln(x, gamma, beta):
    xf = x.astype(jnp.float32)
    mu = jnp.mean(xf, -1, keepdims=True)
    var = jnp.mean(jnp.square(xf - mu), -1, keepdims=True)
    y = (xf - mu) * lax.rsqrt(var + LN_EPS) * gamma.astype(jnp.float32) + beta.astype(jnp.float32)
    return y.astype(x.dtype)


def _rms_norm(x, g):
    xf = x.astype(jnp.float32)
    y = xf * lax.rsqrt(jnp.mean(jnp.square(xf), -1, keepdims=True) + RMS_EPS)
    return (y * g.astype(jnp.float32)).astype(x.dtype)


def _modulation(cond, w_ada, b_ada):
    m = (jax.nn.silu(cond) @ w_ada + b_ada)[:, None, :]
    return jnp.split(m, N_MOD, axis=-1)


def _modulate(x, shift, scale):
    return _plain_ln(x) * (1.0 + scale) + shift


def _post_norm(x, delta, gamma, beta):
    return _affine_ln(DEEPNORM_ALPHA * x + delta, gamma, beta)


def _swiglu(u, w_gate, w_up, w_down):
    return (jax.nn.silu(u @ w_gate) * (u @ w_up)) @ w_down


def _axial_rope(n_tokens):
    rows = n_tokens // GRID_W
    row_id = jnp.repeat(jnp.arange(rows, dtype=jnp.float32), GRID_W)
    col_id = jnp.tile(jnp.arange(GRID_W, dtype=jnp.float32), rows)
    n_freq = QK_ROPE_DIM // 4
    inv_freq = ROPE_THETA ** (-jnp.arange(n_freq, dtype=jnp.float32) / n_freq)
    ang = jnp.concatenate([row_id[:, None] * inv_freq, col_id[:, None] * inv_freq], -1)
    return jnp.cos(ang), jnp.sin(ang)


def _apply_rope(x, cos, sin):
    xf = x.astype(jnp.float32)
    x1, x2 = xf[..., :QK_ROPE_DIM // 2], xf[..., QK_ROPE_DIM // 2:]
    return jnp.concatenate([x1 * cos - x2 * sin, x1 * sin + x2 * cos], -1).astype(x.dtype)


def _split_in_proj(h):
    i0 = Q_LORA_RANK
    i1 = i0 + KV_LORA_RANK
    i2 = i1 + QK_ROPE_DIM
    return h[..., :i0], h[..., i0:i1], h[..., i1:i2], h[..., i2:]


def _mla_queries(c_q, g_q, w_uq):
    b, t, _ = c_q.shape
    q = (_rms_norm(c_q, g_q) @ w_uq).reshape(b, t, N_HEADS, QK_NOPE_DIM + QK_ROPE_DIM)
    return q[..., :QK_NOPE_DIM], q[..., QK_NOPE_DIM:]


def _mla_keys_values(c_kv, w_ukv):
    b, l, _ = c_kv.shape
    kv = (c_kv @ w_ukv).reshape(b, l, N_HEADS, QK_NOPE_DIM + V_HEAD_DIM)
    return kv[..., :QK_NOPE_DIM], kv[..., QK_NOPE_DIM:]


def _mla_attention(q_nope, q_rope, k_nope, k_rope, v):
    b, t, h, _ = q_nope.shape
    nb = t // Q_BLOCK
    qn = q_nope.reshape(b, nb, Q_BLOCK, h, QK_NOPE_DIM).transpose(1, 0, 2, 3, 4)
    qr = q_rope.reshape(b, nb, Q_BLOCK, h, QK_ROPE_DIM).transpose(1, 0, 2, 3, 4)

    def one_block(args):
        qn_b, qr_b = args
        s = (jnp.einsum("bqhd,bkhd->bhqk", qn_b, k_nope)
             + jnp.einsum("bqhd,bkd->bhqk", qr_b, k_rope))
        p = jax.nn.softmax(s.astype(jnp.float32) * ATTN_SCALE, axis=-1).astype(v.dtype)
        return jnp.einsum("bhqk,bkhd->bqhd", p, v)

    out = lax.map(one_block, (qn, qr))
    return out.transpose(1, 0, 2, 3, 4).reshape(b, t, h * V_HEAD_DIM)


def _multiscale_pool(x_pool, w_pool, pool_scale):
    b, t, _ = x_pool.shape
    xg = x_pool.reshape(b, t, N_POOL_GROUPS, POOL_GROUP_DIM)
    csum = jnp.cumsum(xg.astype(jnp.float32), axis=1)
    csum = jnp.pad(csum, ((0, 0), (1, 0), (0, 0), (0, 0)))
    pos = jnp.arange(t)
    half = jnp.array(POOL_WINDOWS, dtype=jnp.int32) // 2
    lo = jnp.clip(pos[:, None] - half[None, :], 0, t)
    hi = jnp.clip(pos[:, None] + half[None, :], 0, t)
    g_idx = jnp.arange(N_POOL_GROUPS)[None, :]
    win_sum = csum[:, hi, g_idx, :] - csum[:, lo, g_idx, :]
    count = (hi - lo).astype(jnp.float32)[None, :, :, None]
    pooled = (win_sum / count).astype(x_pool.dtype) - xg
    mixed = jnp.einsum("btgd,gde->btge", pooled, w_pool).reshape(b, t, POOL_WIDTH)
    return mixed * pool_scale


def _context_mixer(u, w_in, g_q, w_uq, g_kv, w_ukv, w_pool, pool_scale, w_out):
    c_q, c_kv, k_rope, x_pool = _split_in_proj(u @ w_in)
    c_kv = _rms_norm(c_kv, g_kv)
    q_nope, q_rope = _mla_queries(c_q, g_q, w_uq)
    k_nope, v = _mla_keys_values(c_kv, w_ukv)
    attn = _mla_attention(q_nope, q_rope, k_nope, k_rope, v)
    pool = _multiscale_pool(x_pool, w_pool, pool_scale)
    return jnp.concatenate([attn, pool], -1) @ w_out, (c_kv, k_rope)


def _latent_mixer(u, ctx_ckv, ctx_krope, cos, sin, w_in, g_q, w_uq, g_kv, w_ukv, w_pool, pool_scale, w_out):
    c_q, c_kv, k_rope, x_pool = _split_in_proj(u @ w_in)
    c_kv = _rms_norm(c_kv, g_kv)
    q_nope, q_rope = _mla_queries(c_q, g_q, w_uq)
    q_rope = _apply_rope(q_rope, cos[:, None, :], sin[:, None, :])
    k_rope = _apply_rope(k_rope, cos, sin)
    k_nope, v = _mla_keys_values(jnp.concatenate([c_kv, ctx_ckv], axis=1), w_ukv)
    k_rope_all = jnp.concatenate([k_rope, ctx_krope], axis=1)
    attn = _mla_attention(q_nope, q_rope, k_nope, k_rope_all, v)
    pool = _multiscale_pool(x_pool, w_pool, pool_scale)
    return jnp.concatenate([attn, pool], -1) @ w_out, None


def _trunk_layer(x, mods, mix_fn, w_gate, w_up, w_down, gamma, beta):
    sh1, sc1, g1, sh2, sc2, g2, sh3, sc3, g3 = mods
    x = _post_norm(x, 0.5 * g1 * _swiglu(_modulate(x, sh1, sc1), w_gate[0], w_up[0], w_down[0]), gamma[0], beta[0])
    mix, aux = mix_fn(_modulate(x, sh2, sc2))
    x = _post_norm(x, g2 * mix, gamma[1], beta[1])
    x = _post_norm(x, 0.5 * g3 * _swiglu(_modulate(x, sh3, sc3), w_gate[1], w_up[1], w_down[1]), gamma[2], beta[2])
    return x, aux


def setup_inputs(seed: int = 0) -> dict:
    key = jax.random.key(seed)
    ks = jax.random.split(key, 21)

    def nrm(k, shape, s):
        return s * jax.random.normal(k, shape, jnp.float32)

    d_q = N_HEADS * (QK_NOPE_DIM + QK_ROPE_DIM)
    d_kv = N_HEADS * (QK_NOPE_DIM + V_HEAD_DIM)
    return {
        "x_prompt": nrm(ks[0], (BATCH, SEQ, D_MODEL), 1.0),
        "x_sample": nrm(ks[1], (DEC_BATCH, DEC_SEQ, D_MODEL), 1.0),
        "cache_ckv": nrm(ks[2], (DEC_BATCH, DEPTH, PAST_LEN, KV_LORA_RANK), 1.0),
        "cache_krope": nrm(ks[3], (DEC_BATCH, DEPTH, PAST_LEN, QK_ROPE_DIM), 1.0),
        "c": nrm(ks[4], (DEC_BATCH, D_MODEL), 1.0),
        "c_ctx": nrm(ks[5], (D_MODEL,), 1.0),
        "w_ada": nrm(ks[6], (DEPTH, D_MODEL, N_MOD * D_MODEL), 0.5 * D_MODEL ** -0.5),
        "b_ada": nrm(ks[7], (DEPTH, N_MOD * D_MODEL), 0.02),
        "w_in": nrm(ks[8], (DEPTH, D_MODEL, IN_PROJ_DIM), D_MODEL ** -0.5),
        "g_q": 1.0 + nrm(ks[9], (DEPTH, Q_LORA_RANK), 0.05),
        "w_uq": nrm(ks[10], (DEPTH, Q_LORA_RANK, d_q), Q_LORA_RANK ** -0.5),
        "g_kv": 1.0 + nrm(ks[11], (DEPTH, KV_LORA_RANK), 0.05),
        "w_ukv": nrm(ks[12], (DEPTH, KV_LORA_RANK, d_kv), KV_LORA_RANK ** -0.5),
        "w_pool": nrm(ks[13], (DEPTH, N_POOL_GROUPS, POOL_GROUP_DIM, POOL_GROUP_DIM), POOL_GROUP_DIM ** -0.5),
        "pool_scale": 0.5 + nrm(ks[14], (DEPTH, POOL_WIDTH), 0.05),
        "w_out": nrm(ks[15], (DEPTH, D_MODEL, D_MODEL), DEEPNORM_BETA * D_MODEL ** -0.5),
        "w_ffn_gate": nrm(ks[16], (DEPTH, 2, D_MODEL, D_FF), D_MODEL ** -0.5),
        "w_ffn_up": nrm(ks[17], (DEPTH, 2, D_MODEL, D_FF), D_MODEL ** -0.5),
        "w_ffn_down": nrm(ks[18], (DEPTH, 2, D_FF, D_MODEL), DEEPNORM_BETA * D_FF ** -0.5),
        "ln_gamma": 1.0 + nrm(ks[19], (DEPTH, 3, D_MODEL), 0.05),
        "ln_beta": nrm(ks[20], (DEPTH, 3, D_MODEL), 0.02),
    }


def reference(x_prompt, x_sample, cache_ckv, cache_krope, c, c_ctx, w_ada, b_ada, w_in, g_q, w_uq,
              g_kv, w_ukv, w_pool, pool_scale, w_out, w_ffn_gate, w_ffn_up, w_ffn_down, ln_gamma, ln_beta):
    cos, sin = _axial_rope(x_sample.shape[1])
    y_prompt = x_prompt
    y_sample = x_sample
    ckv_list = []
    krope_list = []
    for l in range(DEPTH):
        mix_w = (w_in[l], g_q[l], w_uq[l], g_kv[l], w_ukv[l], w_pool[l], pool_scale[l], w_out[l])
        mods_ctx = _modulation(c_ctx[None, :], w_ada[l], b_ada[l])
        y_prompt, (ckv_l, krope_l) = _trunk_layer(
            y_prompt, mods_ctx, lambda u: _context_mixer(u, *mix_w),
            w_ffn_gate[l], w_ffn_up[l], w_ffn_down[l], ln_gamma[l], ln_beta[l])
        ckv_list.append(ckv_l)
        krope_list.append(krope_l)
        mods_lat = _modulation(c, w_ada[l], b_ada[l])
        ctx_ckv = cache_ckv[:, l]
        ctx_krope = cache_krope[:, l]
        y_sample, _ = _trunk_layer(
            y_sample, mods_lat, lambda u: _latent_mixer(u, ctx_ckv, ctx_krope, cos, sin, *mix_w),
            w_ffn_gate[l], w_ffn_up[l], w_ffn_down[l], ln_gamma[l], ln_beta[l])
    state_ckv = jnp.stack(ckv_list, axis=1)
    state_krope = jnp.stack(krope_list, axis=1)
    return (y_prompt, y_sample, state_ckv, state_krope)
```

```python
import functools

import jax
import jax.numpy as jnp
import numpy as np
from jax import lax
from jax.experimental import pallas as pl
from jax.experimental.pallas import tpu as pltpu

F32 = jnp.float32
BF16 = jnp.bfloat16

D_MODEL = 4096
N_HEADS = 16
QK_NOPE_DIM = 128
QK_ROPE_DIM = 64
V_HEAD_DIM = 128
Q_LORA_RANK = 1024
KV_LORA_RANK = 512
POOL_WIDTH = 2048
POOL_WINDOWS = (2, 4, 8, 16)
POOL_GROUP_DIM = POOL_WIDTH // len(POOL_WINDOWS)
D_FF = 11008
GRID_W = 64
ROPE_THETA = 10000.0
LN_EPS = 1e-5
RMS_EPS = 1e-6
N_MOD = 9
DEPTH = 1
DEEPNORM_ALPHA = (2.0 * DEPTH) ** 0.25
ATTN_SCALE = (QK_NOPE_DIM + QK_ROPE_DIM) ** -0.5

LANES = 128
SUBLANES = 8
VMEM_MIB = 1024 * 1024

HEAD_SLOT = 256
D_FF_PAD = 11264
FFN_TM = 512
FFN_TF = 512
LN_ROWS = 32
IN_N = 3840
IN_TM = 512
IN_TN = 1280
OUT_TM = 512
OUT_TN = 1024
MOD_TN = 512
POOL_PAD = 8
ATT_TQ = 256
KV_ROWS = 256


def _cparams(sem, vmem_mib):
    return pltpu.CompilerParams(dimension_semantics=sem, vmem_limit_bytes=vmem_mib * VMEM_MIB)


def _sigmoid(x):
    return 1.0 / (1.0 + jnp.exp(-x))


def _ln(x):
    mu = jnp.mean(x, axis=-1, keepdims=True)
    xc = x - mu
    var = jnp.mean(xc * xc, axis=-1, keepdims=True)
    return xc * lax.rsqrt(var + LN_EPS)


def _row_chunks(n_rows, body):
    def step(r, carry):
        body(pl.ds(pl.multiple_of(r * LN_ROWS, LN_ROWS), LN_ROWS))
        return carry
    lax.fori_loop(0, n_rows // LN_ROWS, step, 0)


def _mod_kernel(cb_ref, w_ref, b_ref, o_ref, s_ref):
    @pl.when(pl.program_id(0) == 0)
    def _():
        c = cb_ref[...]
        s_ref[...] = c * _sigmoid(c)

    tn = w_ref.shape[1]
    rows = []
    cols = [w_ref[:, c0:c0 + LANES] for c0 in range(0, tn, LANES)]
    for r in range(s_ref.shape[0]):
        s = s_ref[r]
        rows.append(jnp.concatenate(
            [jnp.sum(wc * s, axis=0, keepdims=True) for wc in cols], axis=1))
    rows.append(jnp.zeros((SUBLANES - len(rows), tn), F32))
    o_ref[...] = jnp.concatenate(rows, axis=0) + b_ref[...]


def _modulation(cond, w_ada, b_ada):
    n_cond = cond.shape[0]
    n_out = w_ada.shape[1]
    cb = jnp.broadcast_to(cond[:, :, None], (n_cond, D_MODEL, LANES))
    out = pl.pallas_call(
        _mod_kernel,
        grid=(n_out // MOD_TN,),
        in_specs=[
            pl.BlockSpec((n_cond, D_MODEL, LANES), lambda n: (0, 0, 0), pipeline_mode=pl.Buffered(1)),
            pl.BlockSpec((D_MODEL, MOD_TN), lambda n: (0, n)),
            pl.BlockSpec((1, MOD_TN), lambda n: (0, n)),
        ],
        out_specs=pl.BlockSpec((SUBLANES, MOD_TN), lambda n: (0, n)),
        out_shape=jax.ShapeDtypeStruct((SUBLANES, n_out), F32),
        scratch_shapes=[pltpu.VMEM((n_cond, D_MODEL, LANES), F32)],
        compiler_params=_cparams(("arbitrary",), 40),
        name="modulation",
    )(cb, w_ada, b_ada.reshape(1, n_out))
    return out[:n_cond].reshape(n_cond, N_MOD, D_MODEL)


def _ffn_kernel(x_ref, mod_ref, wg_ref, wu_ref, wd_ref, gam_ref, bet_ref, o_ref, u_ref, *, k0, nj):
    j = pl.program_id(1)
    tm = x_ref.shape[0]

    @pl.when(j == 0)
    def _():
        shift = mod_ref[0, k0:k0 + 1, :]
        scale1 = 1.0 + mod_ref[0, k0 + 1:k0 + 2, :]

        def body(rows):
            u_ref[rows, :] = (_ln(x_ref[rows, :]) * scale1 + shift).astype(BF16)
        _row_chunks(tm, body)
        o_ref[...] = jnp.zeros(o_ref.shape, F32)

    u = u_ref[...]
    g = jnp.dot(u, wg_ref[...], preferred_element_type=F32)
    up = jnp.dot(u, wu_ref[...], preferred_element_type=F32)
    h = (g * _sigmoid(g) * up).astype(BF16)
    o_ref[...] += jnp.dot(h, wd_ref[...], preferred_element_type=F32)

    @pl.when(j == nj - 1)
    def _():
        half_gate = 0.5 * mod_ref[0, k0 + 2:k0 + 3, :]
        gam = gam_ref[...]
        bet = bet_ref[...]

        def body(rows):
            t = DEEPNORM_ALPHA * x_ref[rows, :] + half_gate * o_ref[rows, :]
            o_ref[rows, :] = _ln(t) * gam + bet
        _row_chunks(tm, body)


def _ffn(x, mods, group_of_tile, k0, wg, wu, wd, gam, bet, name):
    t = x.shape[0]
    nj = D_FF_PAD // FFN_TF
    return pl.pallas_call(
        functools.partial(_ffn_kernel, k0=k0, nj=nj),
        grid=(t // FFN_TM, nj),
        in_specs=[
            pl.BlockSpec((FFN_TM, D_MODEL), lambda i, j: (i, 0), pipeline_mode=pl.Buffered(1)),
            pl.BlockSpec((1, N_MOD, D_MODEL), lambda i, j: (group_of_tile(i, FFN_TM), 0, 0)),
            pl.BlockSpec((D_MODEL, FFN_TF), lambda i, j: (0, j)),
            pl.BlockSpec((D_MODEL, FFN_TF), lambda i, j: (0, j)),
            pl.BlockSpec((FFN_TF, D_MODEL), lambda i, j: (j, 0)),
            pl.BlockSpec((1, D_MODEL), lambda i, j: (0, 0)),
            pl.BlockSpec((1, D_MODEL), lambda i, j: (0, 0)),
        ],
        out_specs=pl.BlockSpec((FFN_TM, D_MODEL), lambda i, j: (i, 0)),
        out_shape=jax.ShapeDtypeStruct((t, D_MODEL), F32),
        scratch_shapes=[pltpu.VMEM((FFN_TM, D_MODEL), BF16)],
        compiler_params=_cparams(("arbitrary", "arbitrary"), 60),
        name=name,
    )(x, mods, wg, wu, wd, gam, bet)


def _inproj_kernel(x_ref, mod_ref, w_ref, o_ref, u_ref, *, k0):
    @pl.when(pl.program_id(1) == 0)
    def _():
        shift = mod_ref[0, k0:k0 + 1, :]
        scale1 = 1.0 + mod_ref[0, k0 + 1:k0 + 2, :]

        def body(rows):
            u_ref[rows, :] = (_ln(x_ref[rows, :]) * scale1 + shift).astype(BF16)
        _row_chunks(x_ref.shape[0], body)

    o_ref[...] = jnp.dot(u_ref[...], w_ref[...], preferred_element_type=F32)


def _inproj(x, mods, group_of_tile, k0, w_in, name):
    t = x.shape[0]
    return pl.pallas_call(
        functools.partial(_inproj_kernel, k0=k0),
        grid=(t // IN_TM, IN_N // IN_TN),
        in_specs=[
            pl.BlockSpec((IN_TM, D_MODEL), lambda i, j: (i, 0)),
            pl.BlockSpec((1, N_MOD, D_MODEL), lambda i, j: (group_of_tile(i, IN_TM), 0, 0)),
            pl.BlockSpec((D_MODEL, IN_TN), lambda i, j: (0, j)),
        ],
        out_specs=pl.BlockSpec((IN_TM, IN_TN), lambda i, j: (i, j)),
        out_shape=jax.ShapeDtypeStruct((t, IN_N), F32),
        scratch_shapes=[pltpu.VMEM((IN_TM, D_MODEL), BF16)],
        compiler_params=_cparams(("arbitrary", "arbitrary"), 56),
        name=name,
    )(x, mods, w_in)


def _norms_kernel(cq_ref, ckv_ref, kr_ref, gq_ref, gkv_ref, qin_ref, ckvn_ref, krope_ref):
    def rms(x, g):
        return x * lax.rsqrt(jnp.mean(x * x, axis=-1, keepdims=True) + RMS_EPS) * g
    qin_ref[...] = rms(cq_ref[...], gq_ref[...]).astype(BF16)
    ckvn_ref[...] = rms(ckv_ref[...], gkv_ref[...])
    krope_ref[...] = kr_ref[:, :QK_ROPE_DIM]


def _norms(h, g_q, g_kv, name):
    t = h.shape[0]
    tm = 512
    kr_w = 256
    return pl.pallas_call(
        _norms_kernel,
        grid=(t // tm,),
        in_specs=[
            pl.BlockSpec((tm, Q_LORA_RANK), lambda i: (i, POOL_WIDTH // Q_LORA_RANK)),
            pl.BlockSpec((tm, KV_LORA_RANK), lambda i: (i, (POOL_WIDTH + Q_LORA_RANK) // KV_LORA_RANK)),
            pl.BlockSpec((tm, kr_w), lambda i: (i, (POOL_WIDTH + Q_LORA_RANK + KV_LORA_RANK) // kr_w)),
            pl.BlockSpec((1, Q_LORA_RANK), lambda i: (0, 0)),
            pl.BlockSpec((1, KV_LORA_RANK), lambda i: (0, 0)),
        ],
        out_specs=[
            pl.BlockSpec((tm, Q_LORA_RANK), lambda i: (i, 0)),
            pl.BlockSpec((tm, KV_LORA_RANK), lambda i: (i, 0)),
            pl.BlockSpec((tm, QK_ROPE_DIM), lambda i: (i, 0)),
        ],
        out_shape=[
            jax.ShapeDtypeStruct((t, Q_LORA_RANK), BF16),
            jax.ShapeDtypeStruct((t, KV_LORA_RANK), F32),
            jax.ShapeDtypeStruct((t, QK_ROPE_DIM), F32),
        ],
        compiler_params=_cparams(("arbitrary",), 32),
        name=name,
    )(h, h, h, g_q, g_kv)


def _pool_kernel(xp_ref, w_ref, ps_ref, o_ref, pad_ref):
    seq = xp_ref.shape[0]
    zeros = jnp.zeros((POOL_PAD, POOL_WIDTH), F32)
    pad_ref[0:POOL_PAD, :] = zeros
    pad_ref[POOL_PAD + seq:POOL_PAD + seq + POOL_PAD, :] = zeros
    pad_ref[POOL_PAD:POOL_PAD + seq, :] = xp_ref[...]
    pos = lax.broadcasted_iota(jnp.int32, (seq, 1), 0)
    for g, win in enumerate(POOL_WINDOWS):
        half = win // 2
        c0 = g * POOL_GROUP_DIM
        acc = pad_ref[POOL_PAD - half:POOL_PAD - half + seq, c0:c0 + POOL_GROUP_DIM]
        for d in range(-half + 1, half):
            acc = acc + pad_ref[POOL_PAD + d:POOL_PAD + d + seq, c0:c0 + POOL_GROUP_DIM]
        count = (jnp.minimum(pos + half, seq) - jnp.maximum(pos - half, 0)).astype(F32)
        pooled = acc / count - xp_ref[:, c0:c0 + POOL_GROUP_DIM]
        mixed = jnp.dot(pooled.astype(BF16), w_ref[g], preferred_element_type=F32)
        o_ref[:, c0:c0 + POOL_GROUP_DIM] = (mixed * ps_ref[:, c0:c0 + POOL_GROUP_DIM]).astype(BF16)


def _pool(h, seq, w_pool, pool_scale, name):
    t = h.shape[0]
    return pl.pallas_call(
        _pool_kernel,
        grid=(t // seq,),
        in_specs=[
            pl.BlockSpec((seq, POOL_WIDTH), lambda i: (i, 0)),
            pl.BlockSpec(w_pool.shape, lambda i: (0, 0, 0)),
            pl.BlockSpec((1, POOL_WIDTH), lambda i: (0, 0)),
        ],
        out_specs=pl.BlockSpec((seq, POOL_WIDTH), lambda i: (i, 0)),
        out_shape=jax.ShapeDtypeStruct((t, POOL_WIDTH), BF16),
        scratch_shapes=[pltpu.VMEM((seq + 2 * POOL_PAD, POOL_WIDTH), F32)],
        compiler_params=_cparams(("arbitrary",), 56),
        name=name,
    )(h, w_pool, pool_scale)


def _attn_kernel(*refs, la, lb):
    if lb:
        (qin_ref, mq_ref, ckva_ref, kra_ref, ck_ref, sk_ref, ckvb_ref, krb_ref,
         wuq_ref, wukv_ref, o_ref, kv_ref, krd_ref) = refs
    else:
        qin_ref, mq_ref, ckva_ref, kra_ref, wuq_ref, wukv_ref, o_ref, kv_ref, krd_ref = refs

    @pl.when(pl.program_id(1) == 0)
    def _():
        wukv = wukv_ref[...]
        for r0 in range(0, la, KV_ROWS):
            kv_ref[r0:r0 + KV_ROWS, :] = jnp.dot(
                ckva_ref[r0:r0 + KV_ROWS, :].astype(BF16), wukv, preferred_element_type=F32).astype(BF16)
        kr = kra_ref[...]
        if lb:
            kr2 = jnp.concatenate([kr, kr], axis=1)
            kr2 = kr2 * ck_ref[...] + pltpu.roll(kr2, QK_ROPE_DIM // 2, axis=1) * sk_ref[...]
            krd_ref[0:la, :] = kr2.astype(BF16)
            kv_ref[la:la + lb, :] = jnp.dot(
                ckvb_ref[0, 0].astype(BF16), wukv, preferred_element_type=F32).astype(BF16)
            krb = krb_ref[0, 0]
            krd_ref[la:la + lb, :] = jnp.concatenate([krb, krb], axis=1).astype(BF16)
        else:
            krd_ref[...] = jnp.concatenate([kr, jnp.zeros_like(kr)], axis=1).astype(BF16)

    q = jnp.dot(qin_ref[...], wuq_ref[...], preferred_element_type=F32)
    mq = mq_ref[...]
    krd = krd_ref[...]
    for h in range(N_HEADS):
        c0 = h * HEAD_SLOT
        qh = (q[:, c0:c0 + HEAD_SLOT] * mq).astype(BF16)
        kh = jnp.concatenate([kv_ref[:, c0:c0 + QK_NOPE_DIM], krd], axis=1)
        s = lax.dot_general(qh, kh, (((1,), (1,)), ((), ())), preferred_element_type=F32)
        e = jnp.exp(s - jnp.max(s, axis=-1, keepdims=True))
        denom = jnp.sum(e, axis=-1, keepdims=True)
        o = jnp.dot(e.astype(BF16), kv_ref[:, c0 + QK_NOPE_DIM:c0 + HEAD_SLOT], preferred_element_type=F32)
        o_ref[:, h * V_HEAD_DIM:(h + 1) * V_HEAD_DIM] = (o / denom).astype(BF16)


def _attention(q_in, mq, ckv, krope, w_uq, w_ukv, n_batch, la, name, rope=None, cache=None):
    nq = la // ATT_TQ
    lb = 0 if cache is None else cache[0].shape[2]
    mq_rows = mq.shape[0]
    in_specs = [
        pl.BlockSpec((ATT_TQ, Q_LORA_RANK), lambda b, qi: (b * nq + qi, 0)),
        pl.BlockSpec((ATT_TQ, HEAD_SLOT), (lambda b, qi: (qi, 0)) if mq_rows > ATT_TQ else (lambda b, qi: (0, 0))),
        pl.BlockSpec((la, KV_LORA_RANK), lambda b, qi: (b, 0)),
        pl.BlockSpec((la, QK_ROPE_DIM), lambda b, qi: (b, 0)),
    ]
    args = [q_in, mq, ckv, krope]
    if lb:
        ck, sk = rope
        cache_ckv, cache_krope = cache
        in_specs += [
            pl.BlockSpec((la, LANES), lambda b, qi: (0, 0)),
            pl.BlockSpec((la, LANES), lambda b, qi: (0, 0)),
            pl.BlockSpec((1, 1, lb, KV_LORA_RANK), lambda b, qi: (b, 0, 0, 0)),
            pl.BlockSpec((1, 1, lb, QK_ROPE_DIM), lambda b, qi: (b, 0, 0, 0)),
        ]
        args += [ck, sk, cache_ckv, cache_krope]
    in_specs += [
        pl.BlockSpec(w_uq.shape, lambda b, qi: (0, 0), pipeline_mode=pl.Buffered(1)),
        pl.BlockSpec(w_ukv.shape, lambda b, qi: (0, 0), pipeline_mode=pl.Buffered(1)),
    ]
    args += [w_uq, w_ukv]
    return pl.pallas_call(
        functools.partial(_attn_kernel, la=la, lb=lb),
        grid=(n_batch, nq),
        in_specs=in_specs,
        out_specs=pl.BlockSpec((ATT_TQ, N_HEADS * V_HEAD_DIM), lambda b, qi: (b * nq + qi, 0)),
        out_shape=jax.ShapeDtypeStruct((n_batch * la, N_HEADS * V_HEAD_DIM), BF16),
        scratch_shapes=[
            pltpu.VMEM((la + lb, N_HEADS * HEAD_SLOT), BF16),
            pltpu.VMEM((la + lb, LANES), BF16),
        ],
        compiler_params=_cparams(("arbitrary", "arbitrary"), 56),
        name=name,
    )(*args)


def _outproj_kernel(a_ref, p_ref, wa_ref, wp_ref, x_ref, mod_ref, gam_ref, bet_ref, o_ref, *, k0, nj):
    j = pl.program_id(1)
    tn = wa_ref.shape[1]
    mix = (jnp.dot(a_ref[...], wa_ref[...], preferred_element_type=F32)
           + jnp.dot(p_ref[...], wp_ref[...], preferred_element_type=F32))
    for jj in range(nj):
        @pl.when(j == jj)
        def _(jj=jj):
            o_ref[:, jj * tn:(jj + 1) * tn] = mix

    @pl.when(j == nj - 1)
    def _():
        gate = mod_ref[0, k0 + 2:k0 + 3, :]
        gam = gam_ref[...]
        bet = bet_ref[...]

        def body(rows):
            t = DEEPNORM_ALPHA * x_ref[rows, :] + gate * o_ref[rows, :]
            o_ref[rows, :] = _ln(t) * gam + bet
        _row_chunks(x_ref.shape[0], body)


def _outproj(attn, pool, w_out, x, mods, group_of_tile, k0, gam, bet, name):
    t = x.shape[0]
    nj = D_MODEL // OUT_TN
    half = attn.shape[1]
    return pl.pallas_call(
        functools.partial(_outproj_kernel, k0=k0, nj=nj),
        grid=(t // OUT_TM, nj),
        in_specs=[
            pl.BlockSpec((OUT_TM, half), lambda i, j: (i, 0)),
            pl.BlockSpec((OUT_TM, half), lambda i, j: (i, 0)),
            pl.BlockSpec((half, OUT_TN), lambda i, j: (0, j)),
            pl.BlockSpec((half, OUT_TN), lambda i, j: (1, j)),
            pl.BlockSpec((OUT_TM, D_MODEL), lambda i, j: (i, 0), pipeline_mode=pl.Buffered(1)),
            pl.BlockSpec((1, N_MOD, D_MODEL), lambda i, j: (group_of_tile(i, OUT_TM), 0, 0)),
            pl.BlockSpec((1, D_MODEL), lambda i, j: (0, 0)),
            pl.BlockSpec((1, D_MODEL), lambda i, j: (0, 0)),
        ],
        out_specs=pl.BlockSpec((OUT_TM, D_MODEL), lambda i, j: (i, 0)),
        out_shape=jax.ShapeDtypeStruct((t, D_MODEL), F32),
        compiler_params=_cparams(("arbitrary", "arbitrary"), 56),
        name=name,
    )(attn, pool, w_out, w_out, x, mods, gam, bet)


def _pad_cols(w, n):
    return jnp.pad(w, ((0, 0), (0, n - w.shape[1])))


def _uq_column_order():
    head = QK_NOPE_DIM + QK_ROPE_DIM
    half = QK_ROPE_DIM // 2
    cols = []
    for h in range(N_HEADS):
        base = h * head
        cols += list(range(base, base + head))
        cols += list(range(base + QK_NOPE_DIM + half, base + head))
        cols += list(range(base + QK_NOPE_DIM, base + QK_NOPE_DIM + half))
    return np.asarray(cols, dtype=np.int32)


def _rope_tables(n_tokens):
    rows = n_tokens // GRID_W
    row_id = jnp.repeat(jnp.arange(rows, dtype=F32), GRID_W)
    col_id = jnp.tile(jnp.arange(GRID_W, dtype=F32), rows)
    n_freq = QK_ROPE_DIM // 4
    inv_freq = ROPE_THETA ** (-jnp.arange(n_freq, dtype=F32) / n_freq)
    ang = jnp.concatenate([row_id[:, None] * inv_freq, col_id[:, None] * inv_freq], -1)
    cos, sin = jnp.cos(ang), jnp.sin(ang)
    cos2 = jnp.concatenate([cos, cos], -1)
    sin2 = jnp.concatenate([-sin, sin], -1)
    ones = jnp.ones((n_tokens, QK_NOPE_DIM), F32)
    mq = ATTN_SCALE * jnp.concatenate([ones, cos2, sin2], -1)
    ck = jnp.concatenate([cos2, cos2], -1)
    sk = jnp.concatenate([sin2, sin2], -1)
    return mq, ck, sk


def _trunk(x, mods, group_of_tile, seq, n_batch, wts, rope=None, cache=None):
    (wg, wu, wd, w_in, g_q, w_uq, g_kv, w_ukv, w_pool, pool_scale, w_out, gam, bet, mq) = wts
    tag = "lat" if cache is not None else "ctx"
    x1 = _ffn(x, mods, group_of_tile, 0, wg[0], wu[0], wd[0], gam[0], bet[0], "ffn1_" + tag)
    h = _inproj(x1, mods, group_of_tile, 3, w_in, "inproj_" + tag)
    q_in, ckv, krope = _norms(h, g_q, g_kv, "norms_" + tag)
    pool = _pool(h, seq, w_pool, pool_scale, "pool_" + tag)
    attn = _attention(q_in, mq, ckv, krope, w_uq, w_ukv, n_batch, seq, "attn_" + tag, rope=rope, cache=cache)
    x2 = _outproj(attn, pool, w_out, x1, mods, group_of_tile, 3, gam[1], bet[1], "outproj_" + tag)
    x3 = _ffn(x2, mods, group_of_tile, 6, wg[1], wu[1], wd[1], gam[2], bet[2], "ffn2_" + tag)
    return x3, ckv, krope


def kernel(x_prompt, x_sample, cache_ckv, cache_krope, c, c_ctx, w_ada, b_ada, w_in, g_q, w_uq, g_kv, w_ukv,
           w_pool, pool_scale, w_out, w_ffn_gate, w_ffn_up, w_ffn_down, ln_gamma, ln_beta):
    n_ctx, l_ctx, _ = x_prompt.shape
    n_lat, l_lat, _ = x_sample.shape
    assert w_ada.shape[0] == DEPTH
    lyr = 0

    wg = [_pad_cols(w_ffn_gate[lyr, k].astype(BF16), D_FF_PAD) for k in range(2)]
    wu = [_pad_cols(w_ffn_up[lyr, k].astype(BF16), D_FF_PAD) for k in range(2)]
    wd = [jnp.pad(w_ffn_down[lyr, k].astype(BF16), ((0, D_FF_PAD - D_FF), (0, 0))) for k in range(2)]
    i0 = Q_LORA_RANK
    i1 = i0 + KV_LORA_RANK
    i2 = i1 + QK_ROPE_DIM
    wi = w_in[lyr].astype(BF16)
    w_in_k = _pad_cols(jnp.concatenate([wi[:, i2:], wi[:, :i2]], axis=1), IN_N)
    w_uq_k = w_uq[lyr].astype(BF16)[:, _uq_column_order()]
    w_ukv_k = w_ukv[lyr].astype(BF16)
    w_pool_k = w_pool[lyr].astype(BF16)
    w_out_k = w_out[lyr].astype(BF16)
    gam = [ln_gamma[lyr, k].reshape(1, D_MODEL) for k in range(3)]
    bet = [ln_beta[lyr, k].reshape(1, D_MODEL) for k in range(3)]
    g_q_k = g_q[lyr].reshape(1, Q_LORA_RANK)
    g_kv_k = g_kv[lyr].reshape(1, KV_LORA_RANK)
    ps = pool_scale[lyr].reshape(1, POOL_WIDTH)

    mq_lat, ck, sk = _rope_tables(l_lat)
    mq_ctx = jnp.broadcast_to(
        ATTN_SCALE * jnp.concatenate([jnp.ones((QK_NOPE_DIM + QK_ROPE_DIM,), F32), jnp.zeros((QK_ROPE_DIM,), F32)]),
        (ATT_TQ, HEAD_SLOT))

    mods = _modulation(jnp.concatenate([c_ctx[None, :], c], axis=0), w_ada[lyr], b_ada[lyr])

    base = (wg, wu, wd, w_in_k, g_q_k, w_uq_k, g_kv_k, w_ukv_k, w_pool_k, ps, w_out_k, gam, bet)
    y_ctx, ckv_ctx, krope_ctx = _trunk(
        x_prompt.reshape(n_ctx * l_ctx, D_MODEL), mods, lambda i, tm: 0, l_ctx, n_ctx, base + (mq_ctx,))
    y_lat, _, _ = _trunk(
        x_sample.reshape(n_lat * l_lat, D_MODEL), mods, lambda i, tm: 1 + (i * tm) // l_lat, l_lat, n_lat,
        base + (mq_lat,), rope=(ck, sk), cache=(cache_ckv, cache_krope))

    return (y_ctx.reshape(n_ctx, l_ctx, D_MODEL),
            y_lat.reshape(n_lat, l_lat, D_MODEL),
            ckv_ctx.reshape(n_ctx, DEPTH, l_ctx, KV_LORA_RANK),
            krope_ctx.reshape(n_ctx, DEPTH, l_ctx, QK_ROPE_DIM))
```

```python
import functools

import jax
import jax.numpy as jnp
import numpy as np
from jax import lax
from jax.experimental import pallas as pl
from jax.experimental.pallas import tpu as pltpu

F32 = jnp.float32
BF16 = jnp.bfloat16

D_MODEL = 4096
N_HEADS = 16
QK_NOPE_DIM = 128
QK_ROPE_DIM = 64
V_HEAD_DIM = 128
Q_LORA_RANK = 1024
KV_LORA_RANK = 512
POOL_WIDTH = 2048
POOL_WINDOWS = (2, 4, 8, 16)
POOL_GROUP_DIM = POOL_WIDTH // len(POOL_WINDOWS)
D_FF = 11008
GRID_W = 64
ROPE_THETA = 10000.0
LN_EPS = 1e-5
RMS_EPS = 1e-6
N_MOD = 9
DEPTH = 1
DEEPNORM_ALPHA = (2.0 * DEPTH) ** 0.25
ATTN_SCALE = (QK_NOPE_DIM + QK_ROPE_DIM) ** -0.5

LANES = 128
SUBLANES = 8
VMEM_MIB = 1024 * 1024

HEAD_SLOT = 256
FFN_TM = 512
FFN_TF = 512
LN_ROWS = 32
ROW_CHUNK = 128
IN_N = 3840
IN_TM = 512
IN_TN = 1280
OUT_TM = 512
OUT_TN = 1024
MOD_TN = 512
POOL_PAD = 8
ATT_TQ = 256
KV_ROWS = 256


def _cparams(sem, vmem_mib):
    return pltpu.CompilerParams(dimension_semantics=sem, vmem_limit_bytes=vmem_mib * VMEM_MIB)


def _sigmoid(x):
    return 1.0 / (1.0 + jnp.exp(-x))


def _ln(x):
    mu = jnp.mean(x, axis=-1, keepdims=True)
    xc = x - mu
    var = jnp.mean(xc * xc, axis=-1, keepdims=True)
    return xc * lax.rsqrt(var + LN_EPS)


def _mod_kernel(cb_ref, w_ref, b_ref, o_ref, s_ref):
    @pl.when(pl.program_id(0) == 0)
    def _():
        c = cb_ref[...]
        s_ref[...] = c * _sigmoid(c)

    tn = w_ref.shape[1]
    rows = []
    cols = [w_ref[:, c0:c0 + LANES] for c0 in range(0, tn, LANES)]
    for r in range(s_ref.shape[0]):
        s = s_ref[r]
        rows.append(jnp.concatenate(
            [jnp.sum(wc * s, axis=0, keepdims=True) for wc in cols], axis=1))
    rows.append(jnp.zeros((SUBLANES - len(rows), tn), F32))
    o_ref[...] = jnp.concatenate(rows, axis=0) + b_ref[...]


def _modulation(cond, w_ada, b_ada):
    n_cond = cond.shape[0]
    n_out = w_ada.shape[1]
    cb = jnp.broadcast_to(cond[:, :, None], (n_cond, D_MODEL, LANES))
    out = pl.pallas_call(
        _mod_kernel,
        grid=(n_out // MOD_TN,),
        in_specs=[
            pl.BlockSpec((n_cond, D_MODEL, LANES), lambda n: (0, 0, 0), pipeline_mode=pl.Buffered(1)),
            pl.BlockSpec((D_MODEL, MOD_TN), lambda n: (0, n)),
            pl.BlockSpec((1, MOD_TN), lambda n: (0, n)),
        ],
        out_specs=pl.BlockSpec((SUBLANES, MOD_TN), lambda n: (0, n)),
        out_shape=jax.ShapeDtypeStruct((SUBLANES, n_out), F32),
        scratch_shapes=[pltpu.VMEM((n_cond, D_MODEL, LANES), F32)],
        compiler_params=_cparams(("arbitrary",), 40),
        name="modulation",
    )(cb, w_ada, b_ada.reshape(1, n_out))
    return out[:n_cond].reshape(n_cond, N_MOD, D_MODEL)


def _ffn_kernel(x_ref, mod_ref, wg_ref, wu_ref, wd_ref, gam_ref, bet_ref, o_ref, u_ref, *, k0, nj, tail):
    j = pl.program_id(1)
    tm = x_ref.shape[0]

    def swiglu_down(u, cols=FFN_TF):
        g = jnp.dot(u, wg_ref[:, :cols], preferred_element_type=F32)
        up = jnp.dot(u, wu_ref[:, :cols], preferred_element_type=F32)
        h = (g * _sigmoid(g) * up).astype(BF16)
        return jnp.dot(h, wd_ref[:cols, :], preferred_element_type=F32)

    @pl.when(j == 0)
    def _():
        shift = mod_ref[0, k0:k0 + 1, :]
        scale1 = 1.0 + mod_ref[0, k0 + 1:k0 + 2, :]
        for r0 in range(0, tm, ROW_CHUNK):
            for p0 in range(r0, r0 + ROW_CHUNK, LN_ROWS):
                u_ref[p0:p0 + LN_ROWS, :] = (_ln(x_ref[p0:p0 + LN_ROWS, :]) * scale1 + shift).astype(BF16)
            o_ref[r0:r0 + ROW_CHUNK, :] = swiglu_down(u_ref[r0:r0 + ROW_CHUNK, :])

    @pl.when(jnp.logical_and(j > 0, j < nj - 1))
    def _():
        o_ref[...] += swiglu_down(u_ref[...])

    @pl.when(j == nj - 1)
    def _():
        half_gate = 0.5 * mod_ref[0, k0 + 2:k0 + 3, :]
        gam = gam_ref[...]
        bet = bet_ref[...]
        for r0 in range(0, tm, ROW_CHUNK):
            o_ref[r0:r0 + ROW_CHUNK, :] += swiglu_down(u_ref[r0:r0 + ROW_CHUNK, :], tail)
            for p0 in range(r0, r0 + ROW_CHUNK, LN_ROWS):
                t = DEEPNORM_ALPHA * x_ref[p0:p0 + LN_ROWS, :] + half_gate * o_ref[p0:p0 + LN_ROWS, :]
                o_ref[p0:p0 + LN_ROWS, :] = _ln(t) * gam + bet


def _ffn(x, mods, group_of_tile, k0, which, wg, wu, wd, gam, bet, name):
    t = x.shape[0]
    nj = pl.cdiv(D_FF, FFN_TF)
    return pl.pallas_call(
        functools.partial(_ffn_kernel, k0=k0, nj=nj, tail=D_FF - (nj - 1) * FFN_TF),
        grid=(t // FFN_TM, nj),
        in_specs=[
            pl.BlockSpec((FFN_TM, D_MODEL), lambda i, j: (i, 0), pipeline_mode=pl.Buffered(1)),
            pl.BlockSpec((1, N_MOD, D_MODEL), lambda i, j: (group_of_tile(i, FFN_TM), 0, 0)),
            pl.BlockSpec((None, D_MODEL, FFN_TF), lambda i, j: (which, 0, j)),
            pl.BlockSpec((None, D_MODEL, FFN_TF), lambda i, j: (which, 0, j)),
            pl.BlockSpec((None, FFN_TF, D_MODEL), lambda i, j: (which, j, 0)),
            pl.BlockSpec((1, D_MODEL), lambda i, j: (0, 0)),
            pl.BlockSpec((1, D_MODEL), lambda i, j: (0, 0)),
        ],
        out_specs=pl.BlockSpec((FFN_TM, D_MODEL), lambda i, j: (i, 0)),
        out_shape=jax.ShapeDtypeStruct((t, D_MODEL), F32),
        scratch_shapes=[pltpu.VMEM((FFN_TM, D_MODEL), BF16)],
        compiler_params=_cparams(("arbitrary", "arbitrary"), 62),
        name=name,
    )(x, mods, wg, wu, wd, gam, bet)


def _inproj_kernel(x_ref, mod_ref, w_ref, o_ref, u_ref, *, k0):
    j = pl.program_id(1)
    tm = x_ref.shape[0]

    @pl.when(j == 0)
    def _():
        shift = mod_ref[0, k0:k0 + 1, :]
        scale1 = 1.0 + mod_ref[0, k0 + 1:k0 + 2, :]
        for r0 in range(0, tm, ROW_CHUNK):
            for p0 in range(r0, r0 + ROW_CHUNK, LN_ROWS):
                u_ref[p0:p0 + LN_ROWS, :] = (_ln(x_ref[p0:p0 + LN_ROWS, :]) * scale1 + shift).astype(BF16)
            o_ref[r0:r0 + ROW_CHUNK, :] = jnp.dot(
                u_ref[r0:r0 + ROW_CHUNK, :], w_ref[...], preferred_element_type=F32)

    @pl.when(j > 0)
    def _():
        o_ref[...] = jnp.dot(u_ref[...], w_ref[...], preferred_element_type=F32)


def _inproj(x, mods, group_of_tile, k0, w_in, name):
    t = x.shape[0]
    return pl.pallas_call(
        functools.partial(_inproj_kernel, k0=k0),
        grid=(t // IN_TM, IN_N // IN_TN),
        in_specs=[
            pl.BlockSpec((IN_TM, D_MODEL), lambda i, j: (i, 0)),
            pl.BlockSpec((1, N_MOD, D_MODEL), lambda i, j: (group_of_tile(i, IN_TM), 0, 0)),
            pl.BlockSpec((D_MODEL, IN_TN), lambda i, j: (0, j)),
        ],
        out_specs=pl.BlockSpec((IN_TM, IN_TN), lambda i, j: (i, j)),
        out_shape=jax.ShapeDtypeStruct((t, IN_N), F32),
        scratch_shapes=[pltpu.VMEM((IN_TM, D_MODEL), BF16)],
        compiler_params=_cparams(("arbitrary", "arbitrary"), 56),
        name=name,
    )(x, mods, w_in)


def _norms_kernel(cq_ref, ckv_ref, kr_ref, gq_ref, gkv_ref, qin_ref, ckvn_ref, krope_ref):
    def rms(x, g):
        return x * lax.rsqrt(jnp.mean(x * x, axis=-1, keepdims=True) + RMS_EPS) * g
    qin_ref[...] = rms(cq_ref[...], gq_ref[...]).astype(BF16)
    ckvn_ref[...] = rms(ckv_ref[...], gkv_ref[...])
    krope_ref[...] = kr_ref[:, :QK_ROPE_DIM]


def _norms(h, g_q, g_kv, name):
    t = h.shape[0]
    tm = 512
    kr_w = 256
    return pl.pallas_call(
        _norms_kernel,
        grid=(t // tm,),
        in_specs=[
            pl.BlockSpec((tm, Q_LORA_RANK), lambda i: (i, POOL_WIDTH // Q_LORA_RANK)),
            pl.BlockSpec((tm, KV_LORA_RANK), lambda i: (i, (POOL_WIDTH + Q_LORA_RANK) // KV_LORA_RANK)),
            pl.BlockSpec((tm, kr_w), lambda i: (i, (POOL_WIDTH + Q_LORA_RANK + KV_LORA_RANK) // kr_w)),
            pl.BlockSpec((1, Q_LORA_RANK), lambda i: (0, 0)),
            pl.BlockSpec((1, KV_LORA_RANK), lambda i: (0, 0)),
        ],
        out_specs=[
            pl.BlockSpec((tm, Q_LORA_RANK), lambda i: (i, 0)),
            pl.BlockSpec((tm, KV_LORA_RANK), lambda i: (i, 0)),
            pl.BlockSpec((tm, QK_ROPE_DIM), lambda i: (i, 0)),
        ],
        out_shape=[
            jax.ShapeDtypeStruct((t, Q_LORA_RANK), BF16),
            jax.ShapeDtypeStruct((t, KV_LORA_RANK), F32),
            jax.ShapeDtypeStruct((t, QK_ROPE_DIM), F32),
        ],
        compiler_params=_cparams(("arbitrary",), 32),
        name=name,
    )(h, h, h, g_q, g_kv)


def _pool_kernel(xp_ref, w_ref, ps_ref, o_ref, pad_ref):
    seq = xp_ref.shape[0]
    zeros = jnp.zeros((POOL_PAD, POOL_WIDTH), F32)
    pad_ref[0:POOL_PAD, :] = zeros
    pad_ref[POOL_PAD + seq:POOL_PAD + seq + POOL_PAD, :] = zeros
    pad_ref[POOL_PAD:POOL_PAD + seq, :] = xp_ref[...]
    pos = lax.broadcasted_iota(jnp.int32, (seq, 1), 0)
    for g, win in enumerate(POOL_WINDOWS):
        half = win // 2
        c0 = g * POOL_GROUP_DIM
        acc = pad_ref[POOL_PAD - half:POOL_PAD - half + seq, c0:c0 + POOL_GROUP_DIM]
        for d in range(-half + 1, half):
            acc = acc + pad_ref[POOL_PAD + d:POOL_PAD + d + seq, c0:c0 + POOL_GROUP_DIM]
        count = (jnp.minimum(pos + half, seq) - jnp.maximum(pos - half, 0)).astype(F32)
        pooled = acc / count - xp_ref[:, c0:c0 + POOL_GROUP_DIM]
        mixed = jnp.dot(pooled.astype(BF16), w_ref[g], preferred_element_type=F32)
        o_ref[:, c0:c0 + POOL_GROUP_DIM] = (mixed * ps_ref[:, c0:c0 + POOL_GROUP_DIM]).astype(BF16)


def _pool(h, seq, w_pool, pool_scale, name):
    t = h.shape[0]
    return pl.pallas_call(
        _pool_kernel,
        grid=(t // seq,),
        in_specs=[
            pl.BlockSpec((seq, POOL_WIDTH), lambda i: (i, 0)),
            pl.BlockSpec(w_pool.shape, lambda i: (0, 0, 0)),
            pl.BlockSpec((1, POOL_WIDTH), lambda i: (0, 0)),
        ],
        out_specs=pl.BlockSpec((seq, POOL_WIDTH), lambda i: (i, 0)),
        out_shape=jax.ShapeDtypeStruct((t, POOL_WIDTH), BF16),
        scratch_shapes=[pltpu.VMEM((seq + 2 * POOL_PAD, POOL_WIDTH), F32)],
        compiler_params=_cparams(("arbitrary",), 56),
        name=name,
    )(h, w_pool, pool_scale)


def _attn_kernel(*refs, la, lb):
    if lb:
        (qin_ref, mq_ref, ckva_ref, kra_ref, ck_ref, sk_ref, ckvb_ref, krb_ref,
         wuq_ref, wukv_ref, o_ref, kv_ref, krd_ref) = refs
    else:
        qin_ref, mq_ref, ckva_ref, kra_ref, wuq_ref, wukv_ref, o_ref, kv_ref, krd_ref = refs

    @pl.when(pl.program_id(1) == 0)
    def _():
        wukv = wukv_ref[...]
        for r0 in range(0, la, KV_ROWS):
            kv_ref[r0:r0 + KV_ROWS, :] = jnp.dot(
                ckva_ref[r0:r0 + KV_ROWS, :].astype(BF16), wukv, preferred_element_type=F32).astype(BF16)
        kr = kra_ref[...]
        if lb:
            kr2 = jnp.concatenate([kr, kr], axis=1)
            kr2 = kr2 * ck_ref[...] + pltpu.roll(kr2, QK_ROPE_DIM // 2, axis=1) * sk_ref[...]
            krd_ref[0:la, :] = kr2.astype(BF16)
            kv_ref[la:la + lb, :] = jnp.dot(
                ckvb_ref[0, 0].astype(BF16), wukv, preferred_element_type=F32).astype(BF16)
            krb = krb_ref[0, 0]
            krd_ref[la:la + lb, :] = jnp.concatenate([krb, krb], axis=1).astype(BF16)
        else:
            krd_ref[...] = jnp.concatenate([kr, jnp.zeros_like(kr)], axis=1).astype(BF16)

    q = jnp.dot(qin_ref[...], wuq_ref[...], preferred_element_type=F32)
    mq = mq_ref[...]
    krd = krd_ref[...]
    for h in range(N_HEADS):
        c0 = h * HEAD_SLOT
        qh = (q[:, c0:c0 + HEAD_SLOT] * mq).astype(BF16)
        kh = jnp.concatenate([kv_ref[:, c0:c0 + QK_NOPE_DIM], krd], axis=1)
        s = lax.dot_general(qh, kh, (((1,), (1,)), ((), ())), preferred_element_type=F32)
        e = jnp.exp(s - jnp.max(s, axis=-1, keepdims=True))
        denom = jnp.sum(e, axis=-1, keepdims=True)
        o = jnp.dot(e.astype(BF16), kv_ref[:, c0 + QK_NOPE_DIM:c0 + HEAD_SLOT], preferred_element_type=F32)
        o_ref[:, h * V_HEAD_DIM:(h + 1) * V_HEAD_DIM] = (o / denom).astype(BF16)


def _attention(q_in, mq, ckv, krope, w_uq, w_ukv, n_batch, la, name, rope=None, cache=None):
    nq = la // ATT_TQ
    lb = 0 if cache is None else cache[0].shape[2]
    mq_rows = mq.shape[0]
    in_specs = [
        pl.BlockSpec((ATT_TQ, Q_LORA_RANK), lambda b, qi: (b * nq + qi, 0)),
        pl.BlockSpec((ATT_TQ, HEAD_SLOT), (lambda b, qi: (qi, 0)) if mq_rows > ATT_TQ else (lambda b, qi: (0, 0))),
        pl.BlockSpec((la, KV_LORA_RANK), lambda b, qi: (b, 0)),
        pl.BlockSpec((la, QK_ROPE_DIM), lambda b, qi: (b, 0)),
    ]
    args = [q_in, mq, ckv, krope]
    if lb:
        ck, sk = rope
        cache_ckv, cache_krope = cache
        in_specs += [
            pl.BlockSpec((la, LANES), lambda b, qi: (0, 0)),
            pl.BlockSpec((la, LANES), lambda b, qi: (0, 0)),
            pl.BlockSpec((1, 1, lb, KV_LORA_RANK), lambda b, qi: (b, 0, 0, 0)),
            pl.BlockSpec((1, 1, lb, QK_ROPE_DIM), lambda b, qi: (b, 0, 0, 0)),
        ]
        args += [ck, sk, cache_ckv, cache_krope]
    in_specs += [
        pl.BlockSpec(w_uq.shape, lambda b, qi: (0, 0), pipeline_mode=pl.Buffered(1)),
        pl.BlockSpec(w_ukv.shape, lambda b, qi: (0, 0), pipeline_mode=pl.Buffered(1)),
    ]
    args += [w_uq, w_ukv]
    return pl.pallas_call(
        functools.partial(_attn_kernel, la=la, lb=lb),
        grid=(n_batch, nq),
        in_specs=in_specs,
        out_specs=pl.BlockSpec((ATT_TQ, N_HEADS * V_HEAD_DIM), lambda b, qi: (b * nq + qi, 0)),
        out_shape=jax.ShapeDtypeStruct((n_batch * la, N_HEADS * V_HEAD_DIM), BF16),
        scratch_shapes=[
            pltpu.VMEM((la + lb, N_HEADS * HEAD_SLOT), BF16),
            pltpu.VMEM((la + lb, LANES), BF16),
        ],
        compiler_params=_cparams(("arbitrary", "arbitrary"), 56),
        name=name,
    )(*args)


def _outproj_kernel(a_ref, p_ref, wa_ref, wp_ref, x_ref, mod_ref, gam_ref, bet_ref, o_ref, *, k0, nj):
    j = pl.program_id(1)
    tm = x_ref.shape[0]
    tn = wa_ref.shape[1]

    def mix(rows):
        return (jnp.dot(a_ref[rows, :], wa_ref[...], preferred_element_type=F32)
                + jnp.dot(p_ref[rows, :], wp_ref[...], preferred_element_type=F32))

    for jj in range(nj - 1):
        @pl.when(j == jj)
        def _(jj=jj):
            o_ref[:, jj * tn:(jj + 1) * tn] = mix(slice(None))

    @pl.when(j == nj - 1)
    def _():
        gate = mod_ref[0, k0 + 2:k0 + 3, :]
        gam = gam_ref[...]
        bet = bet_ref[...]
        for r0 in range(0, tm, ROW_CHUNK):
            o_ref[r0:r0 + ROW_CHUNK, (nj - 1) * tn:nj * tn] = mix(slice(r0, r0 + ROW_CHUNK))
            for p0 in range(r0, r0 + ROW_CHUNK, LN_ROWS):
                t = DEEPNORM_ALPHA * x_ref[p0:p0 + LN_ROWS, :] + gate * o_ref[p0:p0 + LN_ROWS, :]
                o_ref[p0:p0 + LN_ROWS, :] = _ln(t) * gam + bet


def _outproj(attn, pool, w_out, x, mods, group_of_tile, k0, gam, bet, name):
    t = x.shape[0]
    nj = D_MODEL // OUT_TN
    half = attn.shape[1]
    return pl.pallas_call(
        functools.partial(_outproj_kernel, k0=k0, nj=nj),
        grid=(t // OUT_TM, nj),
        in_specs=[
            pl.BlockSpec((OUT_TM, half), lambda i, j: (i, 0)),
            pl.BlockSpec((OUT_TM, half), lambda i, j: (i, 0)),
            pl.BlockSpec((half, OUT_TN), lambda i, j: (0, j)),
            pl.BlockSpec((half, OUT_TN), lambda i, j: (1, j)),
            pl.BlockSpec((OUT_TM, D_MODEL), lambda i, j: (i, 0), pipeline_mode=pl.Buffered(1)),
            pl.BlockSpec((1, N_MOD, D_MODEL), lambda i, j: (group_of_tile(i, OUT_TM), 0, 0)),
            pl.BlockSpec((1, D_MODEL), lambda i, j: (0, 0)),
            pl.BlockSpec((1, D_MODEL), lambda i, j: (0, 0)),
        ],
        out_specs=pl.BlockSpec((OUT_TM, D_MODEL), lambda i, j: (i, 0)),
        out_shape=jax.ShapeDtypeStruct((t, D_MODEL), F32),
        compiler_params=_cparams(("arbitrary", "arbitrary"), 62),
        name=name,
    )(attn, pool, w_out, w_out, x, mods, gam, bet)


def _uq_column_order():
    head = QK_NOPE_DIM + QK_ROPE_DIM
    half = QK_ROPE_DIM // 2
    cols = []
    for h in range(N_HEADS):
        base = h * head
        cols += list(range(base, base + head))
        cols += list(range(base + QK_NOPE_DIM + half, base + head))
        cols += list(range(base + QK_NOPE_DIM, base + QK_NOPE_DIM + half))
    return np.asarray(cols, dtype=np.int32)


def _rope_tables(n_tokens):
    rows = n_tokens // GRID_W
    row_id = jnp.repeat(jnp.arange(rows, dtype=F32), GRID_W)
    col_id = jnp.tile(jnp.arange(GRID_W, dtype=F32), rows)
    n_freq = QK_ROPE_DIM // 4
    inv_freq = ROPE_THETA ** (-jnp.arange(n_freq, dtype=F32) / n_freq)
    ang = jnp.concatenate([row_id[:, None] * inv_freq, col_id[:, None] * inv_freq], -1)
    cos, sin = jnp.cos(ang), jnp.sin(ang)
    cos2 = jnp.concatenate([cos, cos], -1)
    sin2 = jnp.concatenate([-sin, sin], -1)
    ones = jnp.ones((n_tokens, QK_NOPE_DIM), F32)
    mq = ATTN_SCALE * jnp.concatenate([ones, cos2, sin2], -1)
    ck = jnp.concatenate([cos2, cos2], -1)
    sk = jnp.concatenate([sin2, sin2], -1)
    return mq, ck, sk


def _trunk(x, mods, group_of_tile, seq, n_batch, wts, rope=None, cache=None):
    (wg, wu, wd, w_in, g_q, w_uq, g_kv, w_ukv, w_pool, pool_scale, w_out, gam, bet, mq) = wts
    tag = "lat" if cache is not None else "ctx"
    x1 = _ffn(x, mods, group_of_tile, 0, 0, wg, wu, wd, gam[0], bet[0], "ffn1_" + tag)
    h = _inproj(x1, mods, group_of_tile, 3, w_in, "inproj_" + tag)
    q_in, ckv, krope = _norms(h, g_q, g_kv, "norms_" + tag)
    pool = _pool(h, seq, w_pool, pool_scale, "pool_" + tag)
    attn = _attention(q_in, mq, ckv, krope, w_uq, w_ukv, n_batch, seq, "attn_" + tag, rope=rope, cache=cache)
    x2 = _outproj(attn, pool, w_out, x1, mods, group_of_tile, 3, gam[1], bet[1], "outproj_" + tag)
    x3 = _ffn(x2, mods, group_of_tile, 6, 1, wg, wu, wd, gam[2], bet[2], "ffn2_" + tag)
    return x3, ckv, krope


def kernel(x_prompt, x_sample, cache_ckv, cache_krope, c, c_ctx, w_ada, b_ada, w_in, g_q, w_uq, g_kv, w_ukv,
           w_pool, pool_scale, w_out, w_ffn_gate, w_ffn_up, w_ffn_down, ln_gamma, ln_beta):
    n_ctx, l_ctx, _ = x_prompt.shape
    n_lat, l_lat, _ = x_sample.shape
    assert w_ada.shape[0] == DEPTH
    lyr = 0

    wg = w_ffn_gate[lyr].astype(BF16)
    wu = w_ffn_up[lyr].astype(BF16)
    wd = w_ffn_down[lyr].astype(BF16)
    i0 = Q_LORA_RANK
    i1 = i0 + KV_LORA_RANK
    i2 = i1 + QK_ROPE_DIM
    wi = w_in[lyr]
    w_in_k = jnp.concatenate(
        [wi[:, i2:], wi[:, :i2], jnp.zeros((D_MODEL, IN_N - wi.shape[1]), F32)], axis=1).astype(BF16)
    w_uq_k = w_uq[lyr].astype(BF16)[:, _uq_column_order()]
    w_ukv_k = w_ukv[lyr].astype(BF16)
    w_pool_k = w_pool[lyr].astype(BF16)
    w_out_k = w_out[lyr].astype(BF16)
    gam = [ln_gamma[lyr, k].reshape(1, D_MODEL) for k in range(3)]
    bet = [ln_beta[lyr, k].reshape(1, D_MODEL) for k in range(3)]
    g_q_k = g_q[lyr].reshape(1, Q_LORA_RANK)
    g_kv_k = g_kv[lyr].reshape(1, KV_LORA_RANK)
    ps = pool_scale[lyr].reshape(1, POOL_WIDTH)

    mq_lat, ck, sk = _rope_tables(l_lat)
    mq_ctx = jnp.broadcast_to(
        ATTN_SCALE * jnp.concatenate([jnp.ones((QK_NOPE_DIM + QK_ROPE_DIM,), F32), jnp.zeros((QK_ROPE_DIM,), F32)]),
        (ATT_TQ, HEAD_SLOT))

    mods = _modulation(jnp.concatenate([c_ctx[None, :], c], axis=0), w_ada[lyr], b_ada[lyr])

    base = (wg, wu, wd, w_in_k, g_q_k, w_uq_k, g_kv_k, w_ukv_k, w_pool_k, ps, w_out_k, gam, bet)
    y_ctx, ckv_ctx, krope_ctx = _trunk(
        x_prompt.reshape(n_ctx * l_ctx, D_MODEL), mods, lambda i, tm: 0, l_ctx, n_ctx, base + (mq_ctx,))
    y_lat, _, _ = _trunk(
        x_sample.reshape(n_lat * l_lat, D_MODEL), mods, lambda i, tm: 1 + (i * tm) // l_lat, l_lat, n_lat,
        base + (mq_lat,), rope=(ck, sk), cache=(cache_ckv, cache_krope))

    return (y_ctx.reshape(n_ctx, l_ctx, D_MODEL),
            y_lat.reshape(n_lat, l_lat, D_MODEL),
            ckv_ctx.reshape(n_ctx, DEPTH, l_ctx, KV_LORA_RANK),
            krope_ctx.reshape(n_ctx, DEPTH, l_ctx, QK_ROPE_DIM))
```

```python
import functools

import jax
import jax.numpy as jnp
import numpy as np
from jax import lax
from jax.experimental import pallas as pl
from jax.experimental.pallas import tpu as pltpu

F32 = jnp.float32
BF16 = jnp.bfloat16

D_MODEL = 4096
N_HEADS = 16
QK_NOPE_DIM = 128
QK_ROPE_DIM = 64
V_HEAD_DIM = 128
Q_LORA_RANK = 1024
KV_LORA_RANK = 512
POOL_WIDTH = 2048
POOL_WINDOWS = (2, 4, 8, 16)
POOL_GROUP_DIM = POOL_WIDTH // len(POOL_WINDOWS)
D_FF = 11008
GRID_W = 64
ROPE_THETA = 10000.0
LN_EPS = 1e-5
RMS_EPS = 1e-6
N_MOD = 9
DEPTH = 1
DEEPNORM_ALPHA = (2.0 * DEPTH) ** 0.25
ATTN_SCALE = (QK_NOPE_DIM + QK_ROPE_DIM) ** -0.5

LANES = 128
VMEM_MIB = 1024 * 1024

HEAD_SLOT = 256
FFN_TM = 512
FFN_TF = 512
FFN_PIECE = 32
FFN_DOWN_TN = 1024
CAST_ROWS_GATE_UP = 16
CAST_ROWS_DOWN = 32
CAST_ROWS_OUT = 32
CAST_CHUNK = 16
LN_ROWS = 32
IN_N = 3840
IN_TM = 1024
IN_TN = 768
IN_PIECE = 256
OUT_TM = 512
OUT_TN = 1024
OUT_PIECE = 128
MOD_TN = 1024
MOD_ROWS = 16
POOL_PAD = 8
ATT_TQ = 256
KV_ROWS = 256


def _cparams(sem, vmem_mib):
    return pltpu.CompilerParams(dimension_semantics=sem, vmem_limit_bytes=vmem_mib * VMEM_MIB)


def _sigmoid(x):
    return 1.0 / (1.0 + jnp.exp(-x))


def _ln(x):
    mu = jnp.mean(x, axis=-1, keepdims=True)
    xc = x - mu
    var = jnp.mean(xc * xc, axis=-1, keepdims=True)
    return xc * lax.rsqrt(var + LN_EPS)


def _and(*conds):
    out = conds[0]
    for c in conds[1:]:
        out = jnp.logical_and(out, c)
    return out


def _mod_kernel(c_ref, w_ref, b_ref, o_ref, s_ref):
    @pl.when(pl.program_id(0) == 0)
    def _():
        c = c_ref[...]
        s_ref[...] = (c * _sigmoid(c)).astype(BF16)

    o_ref[...] = jnp.dot(s_ref[...], w_ref[...].astype(BF16), preferred_element_type=F32) + b_ref[...]


def _modulation(cond, w_ada, b_ada):
    n_cond = cond.shape[0]
    n_out = w_ada.shape[1]
    cond_rows = jnp.pad(cond, ((0, MOD_ROWS - n_cond), (0, 0)))
    out = pl.pallas_call(
        _mod_kernel,
        grid=(n_out // MOD_TN,),
        in_specs=[
            pl.BlockSpec((MOD_ROWS, D_MODEL), lambda n: (0, 0)),
            pl.BlockSpec((D_MODEL, MOD_TN), lambda n: (0, n)),
            pl.BlockSpec((1, MOD_TN), lambda n: (0, n)),
        ],
        out_specs=pl.BlockSpec((MOD_ROWS, MOD_TN), lambda n: (0, n)),
        out_shape=jax.ShapeDtypeStruct((MOD_ROWS, n_out), F32),
        scratch_shapes=[pltpu.VMEM((MOD_ROWS, D_MODEL), BF16)],
        compiler_params=_cparams(("arbitrary",), 52),
        name="modulation",
    )(cond_rows, w_ada, b_ada.reshape(1, n_out))
    return out[:n_cond].reshape(n_cond, N_MOD, D_MODEL)


def _ffn_kernel(*refs, k0, nj, tail, n_tiles, n_cast):
    xn_ref, xp_ref, modn_ref, modp_ref, wg_ref, wu_ref, wd_ref, gam_ref, bet_ref = refs[:9]
    cast_src = refs[9:9 + n_cast]
    o_ref = refs[9 + n_cast]
    cast_dst = refs[10 + n_cast:10 + 2 * n_cast]
    u_cur, u_nxt, acc, acc_done = refs[10 + 2 * n_cast:]
    r = pl.program_id(0)
    j = pl.program_id(1)
    n_pieces = FFN_TM // FFN_PIECE

    def cast_blocks():
        for src, dst in zip(cast_src, cast_dst):
            for s0 in range(0, src.shape[0], CAST_CHUNK):
                dst[s0:s0 + CAST_CHUNK, :] = src[s0:s0 + CAST_CHUNK, :].astype(BF16)

    @pl.when(_and(r == 0, j == 0))
    def _():
        acc[...] = jnp.zeros(acc.shape, F32)
        acc_done[...] = jnp.zeros(acc_done.shape, F32)

    def pieces():
        rows = pl.ds(pl.multiple_of(j * FFN_PIECE, FFN_PIECE), FFN_PIECE)
        shift = modn_ref[0, k0:k0 + 1, :]
        scale1 = 1.0 + modn_ref[0, k0 + 1:k0 + 2, :]
        u_nxt[rows, :] = (_ln(xn_ref[...]) * scale1 + shift).astype(BF16)
        half_gate = 0.5 * modp_ref[0, k0 + 2:k0 + 3, :]
        t = DEEPNORM_ALPHA * xp_ref[...] + half_gate * acc_done[rows, :]
        o_ref[...] = _ln(t) * gam_ref[...] + bet_ref[...]

    def matmuls(cols):
        u = u_cur[...]
        g = jnp.dot(u, wg_ref[:, :cols], preferred_element_type=F32)
        up = jnp.dot(u, wu_ref[:, :cols], preferred_element_type=F32)
        h = (g * _sigmoid(g) * up).astype(BF16)
        for c0 in range(0, D_MODEL, FFN_DOWN_TN):
            acc[:, c0:c0 + FFN_DOWN_TN] += jnp.dot(
                h, wd_ref[:cols, c0:c0 + FFN_DOWN_TN], preferred_element_type=F32)

    def hand_over():
        acc_done[...] = acc[...]
        acc[...] = jnp.zeros(acc.shape, F32)
        u_cur[...] = u_nxt[...]

    main = _and(r >= 1, r <= n_tiles)
    edge = jnp.logical_not(main)

    @pl.when(_and(main, j < n_pieces))
    def _():
        pieces()
        cast_blocks()
        matmuls(wg_ref.shape[1])

    @pl.when(_and(main, j >= n_pieces, j < nj - 1))
    def _():
        cast_blocks()
        matmuls(wg_ref.shape[1])

    @pl.when(_and(main, j == nj - 1))
    def _():
        cast_blocks()
        matmuls(tail)
        hand_over()

    @pl.when(_and(edge, j < n_pieces))
    def _():
        pieces()
        cast_blocks()

    if n_cast:
        @pl.when(_and(edge, j >= n_pieces, j < nj - 1))
        def _():
            cast_blocks()

    @pl.when(_and(edge, j == nj - 1))
    def _():
        cast_blocks()
        hand_over()


def _ffn(x, mods, group_of_tile, k0, which, wg, wu, wd, gam, bet, name, cast=(), tf=FFN_TF):
    t = x.shape[0]
    n_tiles = t // FFN_TM
    n_pieces = FFN_TM // FFN_PIECE
    nj = pl.cdiv(D_FF, tf)
    assert n_pieces <= nj - 1
    n_steps = (n_tiles + 2) * nj

    cast_in_specs, cast_out_specs, cast_out_shapes, cast_args = [], [], [], []
    for src, lead, rows, first in cast:
        n_rows, n_cols = src.shape[-2:]
        n_blocks = n_rows // rows
        assert n_rows % rows == 0 and first + n_blocks <= n_steps

        def blk(r, j, first=first, n_blocks=n_blocks):
            return jnp.clip(r * nj + j - first, 0, n_blocks - 1)

        cast_in_specs.append(pl.BlockSpec(
            (None,) * len(lead) + (rows, n_cols), lambda r, j, lead=lead, blk=blk: lead + (blk(r, j), 0)))
        cast_out_specs.append(pl.BlockSpec((rows, n_cols), lambda r, j, blk=blk: (blk(r, j), 0)))
        cast_out_shapes.append(jax.ShapeDtypeStruct((n_rows, n_cols), BF16))
        cast_args.append(src)

    def t_next(r):
        return jnp.minimum(r, n_tiles - 1)

    def t_prev(r):
        return jnp.clip(r - 2, 0, n_tiles - 1)

    def piece(j):
        return jnp.minimum(j, n_pieces - 1)

    def w_blk(r, j):
        return jnp.where(r == 0, 0, jnp.where(r == n_tiles + 1, nj - 1, j))

    outs = pl.pallas_call(
        functools.partial(_ffn_kernel, k0=k0, nj=nj, tail=D_FF - (nj - 1) * tf, n_tiles=n_tiles,
                          n_cast=len(cast)),
        grid=(n_tiles + 2, nj),
        in_specs=[
            pl.BlockSpec((FFN_PIECE, D_MODEL), lambda r, j: (t_next(r) * n_pieces + piece(j), 0)),
            pl.BlockSpec((FFN_PIECE, D_MODEL), lambda r, j: (t_prev(r) * n_pieces + piece(j), 0)),
            pl.BlockSpec((1, N_MOD, D_MODEL), lambda r, j: (group_of_tile(t_next(r), FFN_TM), 0, 0)),
            pl.BlockSpec((1, N_MOD, D_MODEL), lambda r, j: (group_of_tile(t_prev(r), FFN_TM), 0, 0)),
            pl.BlockSpec((None, D_MODEL, tf), lambda r, j: (which, 0, w_blk(r, j))),
            pl.BlockSpec((None, D_MODEL, tf), lambda r, j: (which, 0, w_blk(r, j))),
            pl.BlockSpec((None, tf, D_MODEL), lambda r, j: (which, w_blk(r, j), 0)),
            pl.BlockSpec((1, D_MODEL), lambda r, j: (0, 0)),
            pl.BlockSpec((1, D_MODEL), lambda r, j: (0, 0)),
        ] + cast_in_specs,
        out_specs=[pl.BlockSpec((FFN_PIECE, D_MODEL),
                                lambda r, j: (jnp.where(r < 2, 0, t_prev(r) * n_pieces + piece(j)), 0))]
        + cast_out_specs,
        out_shape=[jax.ShapeDtypeStruct((t, D_MODEL), F32)] + cast_out_shapes,
        scratch_shapes=[
            pltpu.VMEM((FFN_TM, D_MODEL), BF16),
            pltpu.VMEM((FFN_TM, D_MODEL), BF16),
            pltpu.VMEM((FFN_TM, D_MODEL), F32),
            pltpu.VMEM((FFN_TM, D_MODEL), F32),
        ],
        compiler_params=_cparams(("arbitrary", "arbitrary"), 62),
        name=name,
    )(x, x, mods, mods, wg, wu, wd, gam, bet, *cast_args)
    return outs


def _inproj_kernel(xn_ref, modn_ref, w_ref, o_ref, u_cur, u_nxt, *, k0, nj):
    r = pl.program_id(0)
    j = pl.program_id(1)
    n_pieces = IN_TM // IN_PIECE

    def pieces():
        shift = modn_ref[0, k0:k0 + 1, :]
        scale1 = 1.0 + modn_ref[0, k0 + 1:k0 + 2, :]
        for s0 in range(0, IN_PIECE, LN_ROWS):
            rows = pl.ds(pl.multiple_of(j * IN_PIECE + s0, LN_ROWS), LN_ROWS)
            u_nxt[rows, :] = (_ln(xn_ref[s0:s0 + LN_ROWS, :]) * scale1 + shift).astype(BF16)

    def matmul():
        o_ref[...] = jnp.dot(u_cur[...], w_ref[...], preferred_element_type=F32)

    def hand_over():
        u_cur[...] = u_nxt[...]

    @pl.when(_and(r >= 1, j < n_pieces))
    def _():
        pieces()
        matmul()

    @pl.when(_and(r >= 1, j >= n_pieces, j < nj - 1))
    def _():
        matmul()

    @pl.when(_and(r >= 1, j == nj - 1))
    def _():
        matmul()
        hand_over()

    @pl.when(_and(r == 0, j < n_pieces))
    def _():
        pieces()

    @pl.when(_and(r == 0, j == nj - 1))
    def _():
        hand_over()


def _inproj(x, mods, group_of_tile, k0, w_in, name):
    t = x.shape[0]
    n_tiles = t // IN_TM
    n_pieces = IN_TM // IN_PIECE
    nj = IN_N // IN_TN
    assert n_pieces <= nj - 1

    def t_next(r):
        return jnp.minimum(r, n_tiles - 1)

    def piece(j):
        return jnp.minimum(j, n_pieces - 1)

    def col_blk(r, j):
        return jnp.where(r == 0, 0, j)

    return pl.pallas_call(
        functools.partial(_inproj_kernel, k0=k0, nj=nj),
        grid=(n_tiles + 1, nj),
        in_specs=[
            pl.BlockSpec((IN_PIECE, D_MODEL), lambda r, j: (t_next(r) * n_pieces + piece(j), 0)),
            pl.BlockSpec((1, N_MOD, D_MODEL), lambda r, j: (group_of_tile(t_next(r), IN_TM), 0, 0)),
            pl.BlockSpec((D_MODEL, IN_TN), lambda r, j: (0, col_blk(r, j))),
        ],
        out_specs=pl.BlockSpec((IN_TM, IN_TN), lambda r, j: (jnp.maximum(r - 1, 0), col_blk(r, j))),
        out_shape=jax.ShapeDtypeStruct((t, IN_N), F32),
        scratch_shapes=[
            pltpu.VMEM((IN_TM, D_MODEL), BF16),
            pltpu.VMEM((IN_TM, D_MODEL), BF16),
        ],
        compiler_params=_cparams(("arbitrary", "arbitrary"), 56),
        name=name,
    )(x, mods, w_in)


def _norms_kernel(cq_ref, ckv_ref, kr_ref, gq_ref, gkv_ref, qin_ref, ckvn_ref, krope_ref):
    def rms(x, g):
        return x * lax.rsqrt(jnp.mean(x * x, axis=-1, keepdims=True) + RMS_EPS) * g
    qin_ref[...] = rms(cq_ref[...], gq_ref[...]).astype(BF16)
    ckvn_ref[...] = rms(ckv_ref[...], gkv_ref[...])
    krope_ref[...] = kr_ref[:, :QK_ROPE_DIM]


def _norms(h, g_q, g_kv, name):
    t = h.shape[0]
    tm = 512
    kr_w = 256
    return pl.pallas_call(
        _norms_kernel,
        grid=(t // tm,),
        in_specs=[
            pl.BlockSpec((tm, Q_LORA_RANK), lambda i: (i, POOL_WIDTH // Q_LORA_RANK)),
            pl.BlockSpec((tm, KV_LORA_RANK), lambda i: (i, (POOL_WIDTH + Q_LORA_RANK) // KV_LORA_RANK)),
            pl.BlockSpec((tm, kr_w), lambda i: (i, (POOL_WIDTH + Q_LORA_RANK + KV_LORA_RANK) // kr_w)),
            pl.BlockSpec((1, Q_LORA_RANK), lambda i: (0, 0)),
            pl.BlockSpec((1, KV_LORA_RANK), lambda i: (0, 0)),
        ],
        out_specs=[
            pl.BlockSpec((tm, Q_LORA_RANK), lambda i: (i, 0)),
            pl.BlockSpec((tm, KV_LORA_RANK), lambda i: (i, 0)),
            pl.BlockSpec((tm, QK_ROPE_DIM), lambda i: (i, 0)),
        ],
        out_shape=[
            jax.ShapeDtypeStruct((t, Q_LORA_RANK), BF16),
            jax.ShapeDtypeStruct((t, KV_LORA_RANK), F32),
            jax.ShapeDtypeStruct((t, QK_ROPE_DIM), F32),
        ],
        compiler_params=_cparams(("arbitrary",), 32),
        name=name,
    )(h, h, h, g_q, g_kv)


def _pool_kernel(xp_ref, w_ref, ps_ref, o_ref, pad_ref):
    seq = xp_ref.shape[0]
    zeros = jnp.zeros((POOL_PAD, POOL_WIDTH), F32)
    pad_ref[0:POOL_PAD, :] = zeros
    pad_ref[POOL_PAD + seq:POOL_PAD + seq + POOL_PAD, :] = zeros
    pad_ref[POOL_PAD:POOL_PAD + seq, :] = xp_ref[...]
    pos = lax.broadcasted_iota(jnp.int32, (seq, 1), 0)
    for g, win in enumerate(POOL_WINDOWS):
        half = win // 2
        c0 = g * POOL_GROUP_DIM
        acc = pad_ref[POOL_PAD - half:POOL_PAD - half + seq, c0:c0 + POOL_GROUP_DIM]
        for d in range(-half + 1, half):
            acc = acc + pad_ref[POOL_PAD + d:POOL_PAD + d + seq, c0:c0 + POOL_GROUP_DIM]
        count = (jnp.minimum(pos + half, seq) - jnp.maximum(pos - half, 0)).astype(F32)
        pooled = acc / count - xp_ref[:, c0:c0 + POOL_GROUP_DIM]
        mixed = jnp.dot(pooled.astype(BF16), w_ref[g], preferred_element_type=F32)
        o_ref[:, c0:c0 + POOL_GROUP_DIM] = (mixed * ps_ref[:, c0:c0 + POOL_GROUP_DIM]).astype(BF16)


def _pool(h, seq, w_pool, pool_scale, name):
    t = h.shape[0]
    return pl.pallas_call(
        _pool_kernel,
        grid=(t // seq,),
        in_specs=[
            pl.BlockSpec((seq, POOL_WIDTH), lambda i: (i, 0)),
            pl.BlockSpec(w_pool.shape, lambda i: (0, 0, 0)),
            pl.BlockSpec((1, POOL_WIDTH), lambda i: (0, 0)),
        ],
        out_specs=pl.BlockSpec((seq, POOL_WIDTH), lambda i: (i, 0)),
        out_shape=jax.ShapeDtypeStruct((t, POOL_WIDTH), BF16),
        scratch_shapes=[pltpu.VMEM((seq + 2 * POOL_PAD, POOL_WIDTH), F32)],
        compiler_params=_cparams(("arbitrary",), 56),
        name=name,
    )(h, w_pool, pool_scale)


def _attn_kernel(*refs, la, lb):
    if lb:
        (qin_ref, mq_ref, ckva_ref, kra_ref, ck_ref, sk_ref, ckvb_ref, krb_ref,
         wuq_ref, wukv_ref, o_ref, kv_ref, krd_ref) = refs
    else:
        qin_ref, mq_ref, ckva_ref, kra_ref, wuq_ref, wukv_ref, o_ref, kv_ref, krd_ref = refs

    @pl.when(pl.program_id(1) == 0)
    def _():
        wukv = wukv_ref[...]
        for r0 in range(0, la, KV_ROWS):
            kv_ref[r0:r0 + KV_ROWS, :] = jnp.dot(
                ckva_ref[r0:r0 + KV_ROWS, :].astype(BF16), wukv, preferred_element_type=F32).astype(BF16)
        kr = kra_ref[...]
        if lb:
            kr2 = jnp.concatenate([kr, kr], axis=1)
            kr2 = kr2 * ck_ref[...] + pltpu.roll(kr2, QK_ROPE_DIM // 2, axis=1) * sk_ref[...]
            krd_ref[0:la, :] = kr2.astype(BF16)
            kv_ref[la:la + lb, :] = jnp.dot(
                ckvb_ref[0, 0].astype(BF16), wukv, preferred_element_type=F32).astype(BF16)
            krb = krb_ref[0, 0]
            krd_ref[la:la + lb, :] = jnp.concatenate([krb, krb], axis=1).astype(BF16)
        else:
            krd_ref[...] = jnp.concatenate([kr, jnp.zeros_like(kr)], axis=1).astype(BF16)

    q = jnp.dot(qin_ref[...], wuq_ref[...], preferred_element_type=F32)
    mq = mq_ref[...]
    krd = krd_ref[...]
    for h in range(N_HEADS):
        c0 = h * HEAD_SLOT
        qh = (q[:, c0:c0 + HEAD_SLOT] * mq).astype(BF16)
        kh = jnp.concatenate([kv_ref[:, c0:c0 + QK_NOPE_DIM], krd], axis=1)
        s = lax.dot_general(qh, kh, (((1,), (1,)), ((), ())), preferred_element_type=F32)
        e = jnp.exp(s - jnp.max(s, axis=-1, keepdims=True))
        denom = jnp.sum(e, axis=-1, keepdims=True)
        o = jnp.dot(e.astype(BF16), kv_ref[:, c0 + QK_NOPE_DIM:c0 + HEAD_SLOT], preferred_element_type=F32)
        o_ref[:, h * V_HEAD_DIM:(h + 1) * V_HEAD_DIM] = (o / denom).astype(BF16)


def _attention(q_in, mq, ckv, krope, w_uq, w_ukv, n_batch, la, name, rope=None, cache=None):
    nq = la // ATT_TQ
    lb = 0 if cache is None else cache[0].shape[2]
    mq_rows = mq.shape[0]
    in_specs = [
        pl.BlockSpec((ATT_TQ, Q_LORA_RANK), lambda b, qi: (b * nq + qi, 0)),
        pl.BlockSpec((ATT_TQ, HEAD_SLOT), (lambda b, qi: (qi, 0)) if mq_rows > ATT_TQ else (lambda b, qi: (0, 0))),
        pl.BlockSpec((la, KV_LORA_RANK), lambda b, qi: (b, 0)),
        pl.BlockSpec((la, QK_ROPE_DIM), lambda b, qi: (b, 0)),
    ]
    args = [q_in, mq, ckv, krope]
    if lb:
        ck, sk = rope
        cache_ckv, cache_krope = cache
        in_specs += [
            pl.BlockSpec((la, LANES), lambda b, qi: (0, 0)),
            pl.BlockSpec((la, LANES), lambda b, qi: (0, 0)),
            pl.BlockSpec((1, 1, lb, KV_LORA_RANK), lambda b, qi: (b, 0, 0, 0)),
            pl.BlockSpec((1, 1, lb, QK_ROPE_DIM), lambda b, qi: (b, 0, 0, 0)),
        ]
        args += [ck, sk, cache_ckv, cache_krope]
    in_specs += [
        pl.BlockSpec(w_uq.shape, lambda b, qi: (0, 0), pipeline_mode=pl.Buffered(1)),
        pl.BlockSpec(w_ukv.shape, lambda b, qi: (0, 0), pipeline_mode=pl.Buffered(1)),
    ]
    args += [w_uq, w_ukv]
    return pl.pallas_call(
        functools.partial(_attn_kernel, la=la, lb=lb),
        grid=(n_batch, nq),
        in_specs=in_specs,
        out_specs=pl.BlockSpec((ATT_TQ, N_HEADS * V_HEAD_DIM), lambda b, qi: (b * nq + qi, 0)),
        out_shape=jax.ShapeDtypeStruct((n_batch * la, N_HEADS * V_HEAD_DIM), BF16),
        scratch_shapes=[
            pltpu.VMEM((la + lb, N_HEADS * HEAD_SLOT), BF16),
            pltpu.VMEM((la + lb, LANES), BF16),
        ],
        compiler_params=_cparams(("arbitrary", "arbitrary"), 56),
        name=name,
    )(*args)


def _outproj_kernel(a_ref, p_ref, wa_ref, wp_ref, xp_ref, modp_ref, gam_ref, bet_ref, o_ref,
                    acc, acc_done, *, k0, nj, n_tiles):
    r = pl.program_id(0)
    j = pl.program_id(1)
    tn = wa_ref.shape[1]

    @pl.when(_and(r == 0, j == 0))
    def _():
        acc_done[...] = jnp.zeros(acc_done.shape, F32)

    def pieces():
        for s0 in range(0, OUT_PIECE, LN_ROWS):
            rows = pl.ds(pl.multiple_of(j * OUT_PIECE + s0, LN_ROWS), LN_ROWS)
            parts = []
            for jj in range(nj):
                cols = slice(jj * tn, (jj + 1) * tn)
                gate = modp_ref[0, k0 + 2:k0 + 3, cols]
                parts.append(DEEPNORM_ALPHA * xp_ref[s0:s0 + LN_ROWS, cols] + gate * acc_done[jj, rows, :])
            mu = sum(jnp.sum(p, axis=-1, keepdims=True) for p in parts) * (1.0 / D_MODEL)
            cen = [p - mu for p in parts]
            var = sum(jnp.sum(c * c, axis=-1, keepdims=True) for c in cen) * (1.0 / D_MODEL)
            rstd = lax.rsqrt(var + LN_EPS)
            for jj in range(nj):
                cols = slice(jj * tn, (jj + 1) * tn)
                o_ref[s0:s0 + LN_ROWS, cols] = cen[jj] * rstd * gam_ref[:, cols] + bet_ref[:, cols]

    def matmul():
        acc[j] = (jnp.dot(a_ref[...], wa_ref[...], preferred_element_type=F32)
                  + jnp.dot(p_ref[...], wp_ref[...], preferred_element_type=F32))

    def hand_over():
        acc_done[...] = acc[...]

    @pl.when(_and(r < n_tiles, j < nj - 1))
    def _():
        pieces()
        matmul()

    @pl.when(_and(r < n_tiles, j == nj - 1))
    def _():
        pieces()
        matmul()
        hand_over()

    @pl.when(r == n_tiles)
    def _():
        pieces()


def _outproj(attn, pool, w_out, x, mods, group_of_tile, k0, gam, bet, name):
    t = x.shape[0]
    n_tiles = t // OUT_TM
    nj = D_MODEL // OUT_TN
    assert OUT_TM // OUT_PIECE == nj
    half = attn.shape[1]

    def t_cur(r):
        return jnp.minimum(r, n_tiles - 1)

    def t_prev(r):
        return jnp.maximum(r - 1, 0)

    def w_blk(r, j):
        return jnp.where(r == n_tiles, nj - 1, j)

    return pl.pallas_call(
        functools.partial(_outproj_kernel, k0=k0, nj=nj, n_tiles=n_tiles),
        grid=(n_tiles + 1, nj),
        in_specs=[
            pl.BlockSpec((OUT_TM, half), lambda r, j: (t_cur(r), 0)),
            pl.BlockSpec((OUT_TM, half), lambda r, j: (t_cur(r), 0)),
            pl.BlockSpec((half, OUT_TN), lambda r, j: (0, w_blk(r, j))),
            pl.BlockSpec((half, OUT_TN), lambda r, j: (1, w_blk(r, j))),
            pl.BlockSpec((OUT_PIECE, D_MODEL), lambda r, j: (t_prev(r) * nj + j, 0)),
            pl.BlockSpec((1, N_MOD, D_MODEL), lambda r, j: (group_of_tile(t_prev(r), OUT_TM), 0, 0)),
            pl.BlockSpec((1, D_MODEL), lambda r, j: (0, 0)),
            pl.BlockSpec((1, D_MODEL), lambda r, j: (0, 0)),
        ],
        out_specs=pl.BlockSpec((OUT_PIECE, D_MODEL), lambda r, j: (jnp.where(r == 0, 0, t_prev(r) * nj + j), 0)),
        out_shape=jax.ShapeDtypeStruct((t, D_MODEL), F32),
        scratch_shapes=[
            pltpu.VMEM((nj, OUT_TM, OUT_TN), F32),
            pltpu.VMEM((nj, OUT_TM, OUT_TN), F32),
        ],
        compiler_params=_cparams(("arbitrary", "arbitrary"), 60),
        name=name,
    )(attn, pool, w_out, w_out, x, mods, gam, bet)


def _uq_column_order():
    head = QK_NOPE_DIM + QK_ROPE_DIM
    half = QK_ROPE_DIM // 2
    cols = []
    for h in range(N_HEADS):
        base = h * head
        cols += list(range(base, base + head))
        cols += list(range(base + QK_NOPE_DIM + half, base + head))
        cols += list(range(base + QK_NOPE_DIM, base + QK_NOPE_DIM + half))
    return np.asarray(cols, dtype=np.int32)


def _rope_tables(n_tokens):
    rows = n_tokens // GRID_W
    row_id = jnp.repeat(jnp.arange(rows, dtype=F32), GRID_W)
    col_id = jnp.tile(jnp.arange(GRID_W, dtype=F32), rows)
    n_freq = QK_ROPE_DIM // 4
    inv_freq = ROPE_THETA ** (-jnp.arange(n_freq, dtype=F32) / n_freq)
    ang = jnp.concatenate([row_id[:, None] * inv_freq, col_id[:, None] * inv_freq], -1)
    cos, sin = jnp.cos(ang), jnp.sin(ang)
    cos2 = jnp.concatenate([cos, cos], -1)
    sin2 = jnp.concatenate([-sin, sin], -1)
    ones = jnp.ones((n_tokens, QK_NOPE_DIM), F32)
    mq = ATTN_SCALE * jnp.concatenate([ones, cos2, sin2], -1)
    ck = jnp.concatenate([cos2, cos2], -1)
    sk = jnp.concatenate([sin2, sin2], -1)
    return mq, ck, sk


def _mixer(x1, mods, group_of_tile, seq, n_batch, wts, tag, rope=None, cache=None):
    (w_in, g_q, w_uq, g_kv, w_ukv, w_pool, pool_scale, w_out, gam, bet, mq) = wts
    h = _inproj(x1, mods, group_of_tile, 3, w_in, "inproj_" + tag)
    q_in, ckv, krope = _norms(h, g_q, g_kv, "norms_" + tag)
    pool = _pool(h, seq, w_pool, pool_scale, "pool_" + tag)
    attn = _attention(q_in, mq, ckv, krope, w_uq, w_ukv, n_batch, seq, "attn_" + tag, rope=rope, cache=cache)
    x2 = _outproj(attn, pool, w_out, x1, mods, group_of_tile, 3, gam, bet, "outproj_" + tag)
    return x2, ckv, krope


def kernel(x_prompt, x_sample, cache_ckv, cache_krope, c, c_ctx, w_ada, b_ada, w_in, g_q, w_uq, g_kv, w_ukv,
           w_pool, pool_scale, w_out, w_ffn_gate, w_ffn_up, w_ffn_down, ln_gamma, ln_beta):
    n_ctx, l_ctx, _ = x_prompt.shape
    n_lat, l_lat, _ = x_sample.shape
    assert w_ada.shape[0] == DEPTH
    lyr = 0

    wg1 = w_ffn_gate[lyr, 0:1].astype(BF16)
    wu1 = w_ffn_up[lyr, 0:1].astype(BF16)
    wd1 = w_ffn_down[lyr, 0:1].astype(BF16)
    i0 = Q_LORA_RANK
    i1 = i0 + KV_LORA_RANK
    i2 = i1 + QK_ROPE_DIM
    wi = w_in[lyr]
    w_in_k = jnp.concatenate(
        [wi[:, i2:], wi[:, :i2], jnp.zeros((D_MODEL, IN_N - wi.shape[1]), F32)], axis=1).astype(BF16)
    w_uq_k = w_uq[lyr].astype(BF16)[:, _uq_column_order()]
    w_ukv_k = w_ukv[lyr].astype(BF16)
    w_pool_k = w_pool[lyr].astype(BF16)
    gam =[ln_gamma[lyr, k].reshape(1, D_MODEL) for k in range(3)]
    bet = [ln_beta[lyr, k].reshape(1, D_MODEL) for k in range(3)]
    g_q_k = g_q[lyr].reshape(1, Q_LORA_RANK)
    g_kv_k = g_kv[lyr].reshape(1, KV_LORA_RANK)
    ps = pool_scale[lyr].reshape(1, POOL_WIDTH)

    mq_lat, ck, sk = _rope_tables(l_lat)
    mq_ctx = jnp.broadcast_to(
        ATTN_SCALE * jnp.concatenate([jnp.ones((QK_NOPE_DIM + QK_ROPE_DIM,), F32), jnp.zeros((QK_ROPE_DIM,), F32)]),
        (ATT_TQ, HEAD_SLOT))

    mods = _modulation(jnp.concatenate([c_ctx[None, :], c], axis=0), w_ada[lyr], b_ada[lyr])

    def grp_ctx(i, tm):
        return 0

    def grp_lat(i, tm):
        return 1 + (i * tm) // l_lat

    x_ctx = x_prompt.reshape(n_ctx * l_ctx, D_MODEL)
    x_lat = x_sample.reshape(n_lat * l_lat, D_MODEL)

    x1_ctx, wg2, wu2, wd2 = _ffn(
        x_ctx, mods, grp_ctx, 0, 0, wg1, wu1, wd1, gam[0], bet[0], "ffn1_ctx",
        cast=((w_ffn_gate, (lyr, 1), CAST_ROWS_GATE_UP, 0),
              (w_ffn_up, (lyr, 1), CAST_ROWS_GATE_UP, 0),
              (w_ffn_down, (lyr, 1), CAST_ROWS_DOWN, 0)))
    x1_lat, w_out_k = _ffn(
        x_lat, mods, grp_lat, 0, 0, wg1, wu1, wd1, gam[0], bet[0], "ffn1_lat",
        cast=((w_out, (lyr,), CAST_ROWS_OUT, 0),))

    mix = (w_in_k, g_q_k, w_uq_k, g_kv_k, w_ukv_k, w_pool_k, ps, w_out_k, gam[1], bet[1])
    x2_ctx, ckv_ctx, krope_ctx = _mixer(x1_ctx, mods, grp_ctx, l_ctx, n_ctx, mix + (mq_ctx,), "ctx")
    x2_lat, _, _ = _mixer(x1_lat, mods, grp_lat, l_lat, n_lat, mix + (mq_lat,), "lat",
                          rope=(ck, sk), cache=(cache_ckv, cache_krope))

    wg2, wu2, wd2 = wg2[None], wu2[None], wd2[None]
    y_ctx, = _ffn(x2_ctx, mods, grp_ctx, 6, 0, wg2, wu2, wd2, gam[2], bet[2], "ffn2_ctx")
    y_lat, = _ffn(x2_lat, mods, grp_lat, 6, 0, wg2, wu2, wd2, gam[2], bet[2], "ffn2_lat")

    return (y_ctx.reshape(n_ctx, l_ctx, D_MODEL),
            y_lat.reshape(n_lat, l_lat, D_MODEL),
            ckv_ctx.reshape(n_ctx, DEPTH, l_ctx, KV_LORA_RANK),
            krope_ctx.reshape(n_ctx, DEPTH, l_ctx, QK_ROPE_DIM))
```

```python
import functools

import jax
import jax.numpy as jnp
import numpy as np
from jax import lax
from jax.experimental import pallas as pl
from jax.experimental.pallas import tpu as pltpu

F32 = jnp.float32
BF16 = jnp.bfloat16

D_MODEL = 4096
N_HEADS = 16
QK_NOPE_DIM = 128
QK_ROPE_DIM = 64
V_HEAD_DIM = 128
Q_LORA_RANK = 1024
KV_LORA_RANK = 512
POOL_WIDTH = 2048
POOL_WINDOWS = (2, 4, 8, 16)
POOL_GROUP_DIM = POOL_WIDTH // len(POOL_WINDOWS)
D_FF = 11008
GRID_W = 64
ROPE_THETA = 10000.0
LN_EPS = 1e-5
RMS_EPS = 1e-6
N_MOD = 9
DEPTH = 1
DEEPNORM_ALPHA = (2.0 * DEPTH) ** 0.25
ATTN_SCALE = (QK_NOPE_DIM + QK_ROPE_DIM) ** -0.5

LANES = 128
VMEM_MIB = 1024 * 1024

HEAD_SLOT = 256
FFN_TM = 512
FFN_TF = 512
FFN_PIECE = 32
FFN_DOWN_TN = 1024
CAST_ROWS_GATE_UP = 16
CAST_ROWS_DOWN = 32
CAST_ROWS_OUT = 32
CAST_CHUNK = 16
LN_ROWS = 32
IN_N = 3840
IN_TM = 1024
IN_TN = 768
IN_PIECE = 256
OUT_TM = 512
OUT_TN = 1024
OUT_PIECE = 128
MOD_TN = 512
MOD_ROWS = 16
MOD_WIN_ROWS = 64
POOL_PAD = 8
ATT_TQ = 256
KV_ROWS = 256


def _cparams(sem, vmem_mib):
    return pltpu.CompilerParams(dimension_semantics=sem, vmem_limit_bytes=vmem_mib * VMEM_MIB)


def _sigmoid(x):
    return 1.0 / (1.0 + jnp.exp(-x))


def _ln(x):
    mu = jnp.mean(x, axis=-1, keepdims=True)
    xc = x - mu
    var = jnp.mean(xc * xc, axis=-1, keepdims=True)
    return xc * lax.rsqrt(var + LN_EPS)


def _and(*conds):
    out = conds[0]
    for c in conds[1:]:
        out = jnp.logical_and(out, c)
    return out


def _mod_kernel(c_ref, w_ref, b_ref, wint_ref, o_ref, winkt_ref, s_ref, *, n_src_blocks):
    n = pl.program_id(0)

    @pl.when(n == 0)
    def _():
        c = c_ref[...]
        s_ref[...] = (c * _sigmoid(c)).astype(BF16)

    o_ref[...] = jnp.dot(s_ref[...], w_ref[...].astype(BF16), preferred_element_type=F32) + b_ref[...]

    @pl.when(n < n_src_blocks)
    def _():
        for s0 in range(0, wint_ref.shape[0], CAST_CHUNK):
            winkt_ref[s0:s0 + CAST_CHUNK, :] = wint_ref[s0:s0 + CAST_CHUNK, :].astype(BF16)

    @pl.when(n >= n_src_blocks)
    def _():
        winkt_ref[...] = jnp.zeros(winkt_ref.shape, BF16)


def _modulation(cond, w_ada, b_ada, w_in_t):
    n_cond = cond.shape[0]
    n_out = w_ada.shape[1]
    n_steps = n_out // MOD_TN
    n_src = w_in_t.shape[0] // MOD_WIN_ROWS
    n_rest = (Q_LORA_RANK + KV_LORA_RANK + QK_ROPE_DIM) // MOD_WIN_ROWS
    n_pool = POOL_WIDTH // MOD_WIN_ROWS
    n_dst = IN_N // MOD_WIN_ROWS
    assert w_in_t.shape[0] == (n_rest + n_pool) * MOD_WIN_ROWS and n_dst <= n_steps

    def dst_blk(n):
        return jnp.where(n < n_rest, n_pool + n, jnp.where(n < n_src, n - n_rest, jnp.minimum(n, n_dst - 1)))

    cond_rows = jnp.pad(cond, ((0, MOD_ROWS - n_cond), (0, 0)))
    out, w_in_kt = pl.pallas_call(
        functools.partial(_mod_kernel, n_src_blocks=n_src),
        grid=(n_steps,),
        in_specs=[
            pl.BlockSpec((MOD_ROWS, D_MODEL), lambda n: (0, 0)),
            pl.BlockSpec((D_MODEL, MOD_TN), lambda n: (0, n)),
            pl.BlockSpec((1, MOD_TN), lambda n: (0, n)),
            pl.BlockSpec((MOD_WIN_ROWS, D_MODEL), lambda n: (jnp.minimum(n, n_src - 1), 0)),
        ],
        out_specs=[
            pl.BlockSpec((MOD_ROWS, MOD_TN), lambda n: (0, n)),
            pl.BlockSpec((MOD_WIN_ROWS, D_MODEL), lambda n: (dst_blk(n), 0)),
        ],
        out_shape=[
            jax.ShapeDtypeStruct((MOD_ROWS, n_out), F32),
            jax.ShapeDtypeStruct((IN_N, D_MODEL), BF16),
        ],
        scratch_shapes=[pltpu.VMEM((MOD_ROWS, D_MODEL), BF16)],
        compiler_params=_cparams(("arbitrary",), 40),
        name="modulation",
    )(cond_rows, w_ada, b_ada.reshape(1, n_out), w_in_t)
    return out[:n_cond].reshape(n_cond, N_MOD, D_MODEL), w_in_kt


def _ffn_kernel(*refs, k0, nj, tail, n_tiles, n_cast):
    xn_ref, xp_ref, modn_ref, modp_ref, wg_ref, wu_ref, wd_ref, gam_ref, bet_ref = refs[:9]
    cast_src = refs[9:9 + n_cast]
    o_ref = refs[9 + n_cast]
    cast_dst = refs[10 + n_cast:10 + 2 * n_cast]
    u_cur, u_nxt, acc, acc_done = refs[10 + 2 * n_cast:]
    r = pl.program_id(0)
    j = pl.program_id(1)
    n_pieces = FFN_TM // FFN_PIECE

    def cast_blocks():
        for src, dst in zip(cast_src, cast_dst):
            for s0 in range(0, src.shape[0], CAST_CHUNK):
                dst[s0:s0 + CAST_CHUNK, :] = src[s0:s0 + CAST_CHUNK, :].astype(BF16)

    @pl.when(_and(r == 0, j == 0))
    def _():
        acc[...] = jnp.zeros(acc.shape, F32)
        acc_done[...] = jnp.zeros(acc_done.shape, F32)

    def pieces():
        rows = pl.ds(pl.multiple_of(j * FFN_PIECE, FFN_PIECE), FFN_PIECE)
        shift = modn_ref[0, k0:k0 + 1, :]
        scale1 = 1.0 + modn_ref[0, k0 + 1:k0 + 2, :]
        u_nxt[rows, :] = (_ln(xn_ref[...]) * scale1 + shift).astype(BF16)
        half_gate = 0.5 * modp_ref[0, k0 + 2:k0 + 3, :]
        t = DEEPNORM_ALPHA * xp_ref[...] + half_gate * acc_done[rows, :]
        o_ref[...] = _ln(t) * gam_ref[...] + bet_ref[...]

    def matmuls(cols):
        u = u_cur[...]
        g = jnp.dot(u, wg_ref[:, :cols], preferred_element_type=F32)
        up = jnp.dot(u, wu_ref[:, :cols], preferred_element_type=F32)
        h = (g * _sigmoid(g) * up).astype(BF16)
        for c0 in range(0, D_MODEL, FFN_DOWN_TN):
            acc[:, c0:c0 + FFN_DOWN_TN] += jnp.dot(
                h, wd_ref[:cols, c0:c0 + FFN_DOWN_TN], preferred_element_type=F32)

    def hand_over():
        acc_done[...] = acc[...]
        acc[...] = jnp.zeros(acc.shape, F32)
        u_cur[...] = u_nxt[...]

    main = _and(r >= 1, r <= n_tiles)
    edge = jnp.logical_not(main)

    @pl.when(_and(main, j < n_pieces))
    def _():
        pieces()
        cast_blocks()
        matmuls(wg_ref.shape[1])

    @pl.when(_and(main, j >= n_pieces, j < nj - 1))
    def _():
        cast_blocks()
        matmuls(wg_ref.shape[1])

    @pl.when(_and(main, j == nj - 1))
    def _():
        cast_blocks()
        matmuls(tail)
        hand_over()

    @pl.when(_and(edge, j < n_pieces))
    def _():
        pieces()
        cast_blocks()

    if n_cast:
        @pl.when(_and(edge, j >= n_pieces, j < nj - 1))
        def _():
            cast_blocks()

    @pl.when(_and(edge, j == nj - 1))
    def _():
        cast_blocks()
        hand_over()


def _ffn(x, mods, group_of_tile, k0, which, wg, wu, wd, gam, bet, name, cast=(), tf=FFN_TF):
    t = x.shape[0]
    n_tiles = t // FFN_TM
    n_pieces = FFN_TM // FFN_PIECE
    nj = pl.cdiv(D_FF, tf)
    assert n_pieces <= nj - 1
    n_steps = (n_tiles + 2) * nj

    cast_in_specs, cast_out_specs, cast_out_shapes, cast_args = [], [], [], []
    for src, lead, rows, first in cast:
        n_rows, n_cols = src.shape[-2:]
        n_blocks = n_rows // rows
        assert n_rows % rows == 0 and first + n_blocks <= n_steps

        def blk(r, j, first=first, n_blocks=n_blocks):
            return jnp.clip(r * nj + j - first, 0, n_blocks - 1)

        cast_in_specs.append(pl.BlockSpec(
            (None,) * len(lead) + (rows, n_cols), lambda r, j, lead=lead, blk=blk: lead + (blk(r, j), 0)))
        cast_out_specs.append(pl.BlockSpec((rows, n_cols), lambda r, j, blk=blk: (blk(r, j), 0)))
        cast_out_shapes.append(jax.ShapeDtypeStruct((n_rows, n_cols), BF16))
        cast_args.append(src)

    def t_next(r):
        return jnp.minimum(r, n_tiles - 1)

    def t_prev(r):
        return jnp.clip(r - 2, 0, n_tiles - 1)

    def piece(j):
        return jnp.minimum(j, n_pieces - 1)

    def w_blk(r, j):
        return jnp.where(r == 0, 0, jnp.where(r == n_tiles + 1, nj - 1, j))

    outs = pl.pallas_call(
        functools.partial(_ffn_kernel, k0=k0, nj=nj, tail=D_FF - (nj - 1) * tf, n_tiles=n_tiles,
                          n_cast=len(cast)),
        grid=(n_tiles + 2, nj),
        in_specs=[
            pl.BlockSpec((FFN_PIECE, D_MODEL), lambda r, j: (t_next(r) * n_pieces + piece(j), 0)),
            pl.BlockSpec((FFN_PIECE, D_MODEL), lambda r, j: (t_prev(r) * n_pieces + piece(j), 0)),
            pl.BlockSpec((1, N_MOD, D_MODEL), lambda r, j: (group_of_tile(t_next(r), FFN_TM), 0, 0)),
            pl.BlockSpec((1, N_MOD, D_MODEL), lambda r, j: (group_of_tile(t_prev(r), FFN_TM), 0, 0)),
            pl.BlockSpec((None, D_MODEL, tf), lambda r, j: (which, 0, w_blk(r, j))),
            pl.BlockSpec((None, D_MODEL, tf), lambda r, j: (which, 0, w_blk(r, j))),
            pl.BlockSpec((None, tf, D_MODEL), lambda r, j: (which, w_blk(r, j), 0)),
            pl.BlockSpec((1, D_MODEL), lambda r, j: (0, 0)),
            pl.BlockSpec((1, D_MODEL), lambda r, j: (0, 0)),
        ] + cast_in_specs,
        out_specs=[pl.BlockSpec((FFN_PIECE, D_MODEL),
                                lambda r, j: (jnp.where(r < 2, 0, t_prev(r) * n_pieces + piece(j)), 0))]
        + cast_out_specs,
        out_shape=[jax.ShapeDtypeStruct((t, D_MODEL), F32)] + cast_out_shapes,
        scratch_shapes=[
            pltpu.VMEM((FFN_TM, D_MODEL), BF16),
            pltpu.VMEM((FFN_TM, D_MODEL), BF16),
            pltpu.VMEM((FFN_TM, D_MODEL), F32),
            pltpu.VMEM((FFN_TM, D_MODEL), F32),
        ],
        compiler_params=_cparams(("arbitrary", "arbitrary"), 62),
        name=name,
    )(x, x, mods, mods, wg, wu, wd, gam, bet, *cast_args)
    return outs


def _inproj_kernel(xn_ref, modn_ref, w_ref, o_ref, u_cur, u_nxt, *, k0, nj):
    r = pl.program_id(0)
    j = pl.program_id(1)
    n_pieces = IN_TM // IN_PIECE

    def pieces():
        shift = modn_ref[0, k0:k0 + 1, :]
        scale1 = 1.0 + modn_ref[0, k0 + 1:k0 + 2, :]
        for s0 in range(0, IN_PIECE, LN_ROWS):
            rows = pl.ds(pl.multiple_of(j * IN_PIECE + s0, LN_ROWS), LN_ROWS)
            u_nxt[rows, :] = (_ln(xn_ref[s0:s0 + LN_ROWS, :]) * scale1 + shift).astype(BF16)

    def matmul():
        o_ref[...] = lax.dot_general(u_cur[...], w_ref[...], (((1,), (1,)), ((), ())),
                                     preferred_element_type=F32)

    def hand_over():
        u_cur[...] = u_nxt[...]

    @pl.when(_and(r >= 1, j < n_pieces))
    def _():
        pieces()
        matmul()

    @pl.when(_and(r >= 1, j >= n_pieces, j < nj - 1))
    def _():
        matmul()

    @pl.when(_and(r >= 1, j == nj - 1))
    def _():
        matmul()
        hand_over()

    @pl.when(_and(r == 0, j < n_pieces))
    def _():
        pieces()

    @pl.when(_and(r == 0, j == nj - 1))
    def _():
        hand_over()


def _inproj(x, mods, group_of_tile, k0, w_in, name):
    t = x.shape[0]
    n_tiles = t // IN_TM
    n_pieces = IN_TM // IN_PIECE
    nj = IN_N // IN_TN
    assert n_pieces <= nj - 1

    def t_next(r):
        return jnp.minimum(r, n_tiles - 1)

    def piece(j):
        return jnp.minimum(j, n_pieces - 1)

    def col_blk(r, j):
        return jnp.where(r == 0, 0, j)

    return pl.pallas_call(
        functools.partial(_inproj_kernel, k0=k0, nj=nj),
        grid=(n_tiles + 1, nj),
        in_specs=[
            pl.BlockSpec((IN_PIECE, D_MODEL), lambda r, j: (t_next(r) * n_pieces + piece(j), 0)),
            pl.BlockSpec((1, N_MOD, D_MODEL), lambda r, j: (group_of_tile(t_next(r), IN_TM), 0, 0)),
            pl.BlockSpec((IN_TN, D_MODEL), lambda r, j: (col_blk(r, j), 0)),
        ],
        out_specs=pl.BlockSpec((IN_TM, IN_TN), lambda r, j: (jnp.maximum(r - 1, 0), col_blk(r, j))),
        out_shape=jax.ShapeDtypeStruct((t, IN_N), F32),
        scratch_shapes=[
            pltpu.VMEM((IN_TM, D_MODEL), BF16),
            pltpu.VMEM((IN_TM, D_MODEL), BF16),
        ],
        compiler_params=_cparams(("arbitrary", "arbitrary"), 56),
        name=name,
    )(x, mods, w_in)


def _norms_kernel(cq_ref, ckv_ref, kr_ref, gq_ref, gkv_ref, qin_ref, ckvn_ref, krope_ref):
    def rms(x, g):
        return x * lax.rsqrt(jnp.mean(x * x, axis=-1, keepdims=True) + RMS_EPS) * g
    qin_ref[...] = rms(cq_ref[...], gq_ref[...]).astype(BF16)
    ckvn_ref[...] = rms(ckv_ref[...], gkv_ref[...])
    krope_ref[...] = kr_ref[:, :QK_ROPE_DIM]


def _norms(h, g_q, g_kv, name):
    t = h.shape[0]
    tm = 512
    kr_w = 256
    return pl.pallas_call(
        _norms_kernel,
        grid=(t // tm,),
        in_specs=[
            pl.BlockSpec((tm, Q_LORA_RANK), lambda i: (i, POOL_WIDTH // Q_LORA_RANK)),
            pl.BlockSpec((tm, KV_LORA_RANK), lambda i: (i, (POOL_WIDTH + Q_LORA_RANK) // KV_LORA_RANK)),
            pl.BlockSpec((tm, kr_w), lambda i: (i, (POOL_WIDTH + Q_LORA_RANK + KV_LORA_RANK) // kr_w)),
            pl.BlockSpec((1, Q_LORA_RANK), lambda i: (0, 0)),
            pl.BlockSpec((1, KV_LORA_RANK), lambda i: (0, 0)),
        ],
        out_specs=[
            pl.BlockSpec((tm, Q_LORA_RANK), lambda i: (i, 0)),
            pl.BlockSpec((tm, KV_LORA_RANK), lambda i: (i, 0)),
            pl.BlockSpec((tm, QK_ROPE_DIM), lambda i: (i, 0)),
        ],
        out_shape=[
            jax.ShapeDtypeStruct((t, Q_LORA_RANK), BF16),
            jax.ShapeDtypeStruct((t, KV_LORA_RANK), F32),
            jax.ShapeDtypeStruct((t, QK_ROPE_DIM), F32),
        ],
        compiler_params=_cparams(("arbitrary",), 32),
        name=name,
    )(h, h, h, g_q, g_kv)


def _pool_kernel(xp_ref, w_ref, ps_ref, o_ref, pad_ref):
    seq = xp_ref.shape[0]
    zeros = jnp.zeros((POOL_PAD, POOL_WIDTH), F32)
    pad_ref[0:POOL_PAD, :] = zeros
    pad_ref[POOL_PAD + seq:POOL_PAD + seq + POOL_PAD, :] = zeros
    pad_ref[POOL_PAD:POOL_PAD + seq, :] = xp_ref[...]
    pos = lax.broadcasted_iota(jnp.int32, (seq, 1), 0)
    for g, win in enumerate(POOL_WINDOWS):
        half = win // 2
        c0 = g * POOL_GROUP_DIM
        acc = pad_ref[POOL_PAD - half:POOL_PAD - half + seq, c0:c0 + POOL_GROUP_DIM]
        for d in range(-half + 1, half):
            acc = acc + pad_ref[POOL_PAD + d:POOL_PAD + d + seq, c0:c0 + POOL_GROUP_DIM]
        count = (jnp.minimum(pos + half, seq) - jnp.maximum(pos - half, 0)).astype(F32)
        pooled = acc / count - xp_ref[:, c0:c0 + POOL_GROUP_DIM]
        mixed = jnp.dot(pooled.astype(BF16), w_ref[g], preferred_element_type=F32)
        o_ref[:, c0:c0 + POOL_GROUP_DIM] = (mixed * ps_ref[:, c0:c0 + POOL_GROUP_DIM]).astype(BF16)


def _pool(h, seq, w_pool, pool_scale, name):
    t = h.shape[0]
    return pl.pallas_call(
        _pool_kernel,
        grid=(t // seq,),
        in_specs=[
            pl.BlockSpec((seq, POOL_WIDTH), lambda i: (i, 0)),
            pl.BlockSpec(w_pool.shape, lambda i: (0, 0, 0)),
            pl.BlockSpec((1, POOL_WIDTH), lambda i: (0, 0)),
        ],
        out_specs=pl.BlockSpec((seq, POOL_WIDTH), lambda i: (i, 0)),
        out_shape=jax.ShapeDtypeStruct((t, POOL_WIDTH), BF16),
        scratch_shapes=[pltpu.VMEM((seq + 2 * POOL_PAD, POOL_WIDTH), F32)],
        compiler_params=_cparams(("arbitrary",), 56),
        name=name,
    )(h, w_pool, pool_scale)


def _attn_kernel(*refs, la, lb):
    if lb:
        (qin_ref, mq_ref, ckva_ref, kra_ref, ck_ref, sk_ref, ckvb_ref, krb_ref,
         wuq_ref, wukv_ref, o_ref, kv_ref, krd_ref) = refs
    else:
        qin_ref, mq_ref, ckva_ref, kra_ref, wuq_ref, wukv_ref, o_ref, kv_ref, krd_ref = refs

    @pl.when(pl.program_id(1) == 0)
    def _():
        wukv = wukv_ref[...]
        for r0 in range(0, la, KV_ROWS):
            kv_ref[r0:r0 + KV_ROWS, :] = jnp.dot(
                ckva_ref[r0:r0 + KV_ROWS, :].astype(BF16), wukv, preferred_element_type=F32).astype(BF16)
        kr = kra_ref[...]
        if lb:
            kr2 = jnp.concatenate([kr, kr], axis=1)
            kr2 = kr2 * ck_ref[...] + pltpu.roll(kr2, QK_ROPE_DIM // 2, axis=1) * sk_ref[...]
            krd_ref[0:la, :] = kr2.astype(BF16)
            kv_ref[la:la + lb, :] = jnp.dot(
                ckvb_ref[0, 0].astype(BF16), wukv, preferred_element_type=F32).astype(BF16)
            krb = krb_ref[0, 0]
            krd_ref[la:la + lb, :] = jnp.concatenate([krb, krb], axis=1).astype(BF16)
        else:
            krd_ref[...] = jnp.concatenate([kr, jnp.zeros_like(kr)], axis=1).astype(BF16)

    q = jnp.dot(qin_ref[...], wuq_ref[...], preferred_element_type=F32)
    mq = mq_ref[...]
    krd = krd_ref[...]
    for h in range(N_HEADS):
        c0 = h * HEAD_SLOT
        qh = (q[:, c0:c0 + HEAD_SLOT] * mq).astype(BF16)
        kh = jnp.concatenate([kv_ref[:, c0:c0 + QK_NOPE_DIM], krd], axis=1)
        s = lax.dot_general(qh, kh, (((1,), (1,)), ((), ())), preferred_element_type=F32)
        e = jnp.exp(s - jnp.max(s, axis=-1, keepdims=True))
        denom = jnp.sum(e, axis=-1, keepdims=True)
        o = jnp.dot(e.astype(BF16), kv_ref[:, c0 + QK_NOPE_DIM:c0 + HEAD_SLOT], preferred_element_type=F32)
        o_ref[:, h * V_HEAD_DIM:(h + 1) * V_HEAD_DIM] = (o / denom).astype(BF16)


def _attention(q_in, mq, ckv, krope, w_uq, w_ukv, n_batch, la, name, rope=None, cache=None):
    nq = la // ATT_TQ
    lb = 0 if cache is None else cache[0].shape[2]
    mq_rows = mq.shape[0]
    in_specs = [
        pl.BlockSpec((ATT_TQ, Q_LORA_RANK), lambda b, qi: (b * nq + qi, 0)),
        pl.BlockSpec((ATT_TQ, HEAD_SLOT), (lambda b, qi: (qi, 0)) if mq_rows > ATT_TQ else (lambda b, qi: (0, 0))),
        pl.BlockSpec((la, KV_LORA_RANK), lambda b, qi: (b, 0)),
        pl.BlockSpec((la, QK_ROPE_DIM), lambda b, qi: (b, 0)),
    ]
    args = [q_in, mq, ckv, krope]
    if lb:
        ck, sk = rope
        cache_ckv, cache_krope = cache
        in_specs += [
            pl.BlockSpec((la, LANES), lambda b, qi: (0, 0)),
            pl.BlockSpec((la, LANES), lambda b, qi: (0, 0)),
            pl.BlockSpec((1, 1, lb, KV_LORA_RANK), lambda b, qi: (b, 0, 0, 0)),
            pl.BlockSpec((1, 1, lb, QK_ROPE_DIM), lambda b, qi: (b, 0, 0, 0)),
        ]
        args += [ck, sk, cache_ckv, cache_krope]
    in_specs += [
        pl.BlockSpec(w_uq.shape, lambda b, qi: (0, 0), pipeline_mode=pl.Buffered(1)),
        pl.BlockSpec(w_ukv.shape, lambda b, qi: (0, 0), pipeline_mode=pl.Buffered(1)),
    ]
    args += [w_uq, w_ukv]
    return pl.pallas_call(
        functools.partial(_attn_kernel, la=la, lb=lb),
        grid=(n_batch, nq),
        in_specs=in_specs,
        out_specs=pl.BlockSpec((ATT_TQ, N_HEADS * V_HEAD_DIM), lambda b, qi: (b * nq + qi, 0)),
        out_shape=jax.ShapeDtypeStruct((n_batch * la, N_HEADS * V_HEAD_DIM), BF16),
        scratch_shapes=[
            pltpu.VMEM((la + lb, N_HEADS * HEAD_SLOT), BF16),
            pltpu.VMEM((la + lb, LANES), BF16),
        ],
        compiler_params=_cparams(("arbitrary", "arbitrary"), 56),
        name=name,
    )(*args)


def _outproj_kernel(a_ref, p_ref, wa_ref, wp_ref, xp_ref, modp_ref, gam_ref, bet_ref, o_ref,
                    acc, acc_done, *, k0, nj, n_tiles):
    r = pl.program_id(0)
    j = pl.program_id(1)
    tn = wa_ref.shape[1]

    @pl.when(_and(r == 0, j == 0))
    def _():
        acc_done[...] = jnp.zeros(acc_done.shape, F32)

    def pieces():
        for s0 in range(0, OUT_PIECE, LN_ROWS):
            rows = pl.ds(pl.multiple_of(j * OUT_PIECE + s0, LN_ROWS), LN_ROWS)
            parts = []
            for jj in range(nj):
                cols = slice(jj * tn, (jj + 1) * tn)
                gate = modp_ref[0, k0 + 2:k0 + 3, cols]
                parts.append(DEEPNORM_ALPHA * xp_ref[s0:s0 + LN_ROWS, cols] + gate * acc_done[jj, rows, :])
            mu = sum(jnp.sum(p, axis=-1, keepdims=True) for p in parts) * (1.0 / D_MODEL)
            cen = [p - mu for p in parts]
            var = sum(jnp.sum(c * c, axis=-1, keepdims=True) for c in cen) * (1.0 / D_MODEL)
            rstd = lax.rsqrt(var + LN_EPS)
            for jj in range(nj):
                cols = slice(jj * tn, (jj + 1) * tn)
                o_ref[s0:s0 + LN_ROWS, cols] = cen[jj] * rstd * gam_ref[:, cols] + bet_ref[:, cols]

    def matmul():
        acc[j] = (jnp.dot(a_ref[...], wa_ref[...], preferred_element_type=F32)
                  + jnp.dot(p_ref[...], wp_ref[...], preferred_element_type=F32))

    def hand_over():
        acc_done[...] = acc[...]

    @pl.when(_and(r < n_tiles, j < nj - 1))
    def _():
        pieces()
        matmul()

    @pl.when(_and(r < n_tiles, j == nj - 1))
    def _():
        pieces()
        matmul()
        hand_over()

    @pl.when(r == n_tiles)
    def _():
        pieces()


def _outproj(attn, pool, w_out, x, mods, group_of_tile, k0, gam, bet, name):
    t = x.shape[0]
    n_tiles = t // OUT_TM
    nj = D_MODEL // OUT_TN
    assert OUT_TM // OUT_PIECE == nj
    half = attn.shape[1]

    def t_cur(r):
        return jnp.minimum(r, n_tiles - 1)

    def t_prev(r):
        return jnp.maximum(r - 1, 0)

    def w_blk(r, j):
        return jnp.where(r == n_tiles, nj - 1, j)

    return pl.pallas_call(
        functools.partial(_outproj_kernel, k0=k0, nj=nj, n_tiles=n_tiles),
        grid=(n_tiles + 1, nj),
        in_specs=[
            pl.BlockSpec((OUT_TM, half), lambda r, j: (t_cur(r), 0)),
            pl.BlockSpec((OUT_TM, half), lambda r, j: (t_cur(r), 0)),
            pl.BlockSpec((half, OUT_TN), lambda r, j: (0, w_blk(r, j))),
            pl.BlockSpec((half, OUT_TN), lambda r, j: (1, w_blk(r, j))),
            pl.BlockSpec((OUT_PIECE, D_MODEL), lambda r, j: (t_prev(r) * nj + j, 0)),
            pl.BlockSpec((1, N_MOD, D_MODEL), lambda r, j: (group_of_tile(t_prev(r), OUT_TM), 0, 0)),
            pl.BlockSpec((1, D_MODEL), lambda r, j: (0, 0)),
            pl.BlockSpec((1, D_MODEL), lambda r, j: (0, 0)),
        ],
        out_specs=pl.BlockSpec((OUT_PIECE, D_MODEL), lambda r, j: (jnp.where(r == 0, 0, t_prev(r) * nj + j), 0)),
        out_shape=jax.ShapeDtypeStruct((t, D_MODEL), F32),
        scratch_shapes=[
            pltpu.VMEM((nj, OUT_TM, OUT_TN), F32),
            pltpu.VMEM((nj, OUT_TM, OUT_TN), F32),
        ],
        compiler_params=_cparams(("arbitrary", "arbitrary"), 60),
        name=name,
    )(attn, pool, w_out, w_out, x, mods, gam, bet)


def _uq_column_order():
    head = QK_NOPE_DIM + QK_ROPE_DIM
    half = QK_ROPE_DIM // 2
    cols = []
    for h in range(N_HEADS):
        base = h * head
        cols += list(range(base, base + head))
        cols += list(range(base + QK_NOPE_DIM + half, base + head))
        cols += list(range(base + QK_NOPE_DIM, base + QK_NOPE_DIM + half))
    return np.asarray(cols, dtype=np.int32)


def _rope_tables(n_tokens):
    rows = n_tokens // GRID_W
    row_id = jnp.repeat(jnp.arange(rows, dtype=F32), GRID_W)
    col_id = jnp.tile(jnp.arange(GRID_W, dtype=F32), rows)
    n_freq = QK_ROPE_DIM // 4
    inv_freq = ROPE_THETA ** (-jnp.arange(n_freq, dtype=F32) / n_freq)
    ang = jnp.concatenate([row_id[:, None] * inv_freq, col_id[:, None] * inv_freq], -1)
    cos, sin = jnp.cos(ang), jnp.sin(ang)
    cos2 = jnp.concatenate([cos, cos], -1)
    sin2 = jnp.concatenate([-sin, sin], -1)
    ones = jnp.ones((n_tokens, QK_NOPE_DIM), F32)
    mq = ATTN_SCALE * jnp.concatenate([ones, cos2, sin2], -1)
    ck = jnp.concatenate([cos2, cos2], -1)
    sk = jnp.concatenate([sin2, sin2], -1)
    return mq, ck, sk


def _mixer(x1, mods, group_of_tile, seq, n_batch, wts, tag, rope=None, cache=None):
    (w_in, g_q, w_uq, g_kv, w_ukv, w_pool, pool_scale, w_out, gam, bet, mq) = wts
    h = _inproj(x1, mods, group_of_tile, 3, w_in, "inproj_" + tag)
    q_in, ckv, krope = _norms(h, g_q, g_kv, "norms_" + tag)
    pool = _pool(h, seq, w_pool, pool_scale, "pool_" + tag)
    attn = _attention(q_in, mq, ckv, krope, w_uq, w_ukv, n_batch, seq, "attn_" + tag, rope=rope, cache=cache)
    x2 = _outproj(attn, pool, w_out, x1, mods, group_of_tile, 3, gam, bet, "outproj_" + tag)
    return x2, ckv, krope


def kernel(x_prompt, x_sample, cache_ckv, cache_krope, c, c_ctx, w_ada, b_ada, w_in, g_q, w_uq, g_kv, w_ukv,
           w_pool, pool_scale, w_out, w_ffn_gate, w_ffn_up, w_ffn_down, ln_gamma, ln_beta):
    n_ctx, l_ctx, _ = x_prompt.shape
    n_lat, l_lat, _ = x_sample.shape
    assert w_ada.shape[0] == DEPTH
    lyr = 0

    wg1 = w_ffn_gate[lyr, 0:1].astype(BF16)
    wu1 = w_ffn_up[lyr, 0:1].astype(BF16)
    wd1 = w_ffn_down[lyr, 0:1].astype(BF16)
    w_uq_k = w_uq[lyr].astype(BF16)[:, _uq_column_order()]
    w_ukv_k = w_ukv[lyr].astype(BF16)
    w_pool_k = w_pool[lyr].astype(BF16)
    gam =[ln_gamma[lyr, k].reshape(1, D_MODEL) for k in range(3)]
    bet = [ln_beta[lyr, k].reshape(1, D_MODEL) for k in range(3)]
    g_q_k = g_q[lyr].reshape(1, Q_LORA_RANK)
    g_kv_k = g_kv[lyr].reshape(1, KV_LORA_RANK)
    ps = pool_scale[lyr].reshape(1, POOL_WIDTH)

    mq_lat, ck, sk = _rope_tables(l_lat)
    mq_ctx = jnp.broadcast_to(
        ATTN_SCALE * jnp.concatenate([jnp.ones((QK_NOPE_DIM + QK_ROPE_DIM,), F32), jnp.zeros((QK_ROPE_DIM,), F32)]),
        (ATT_TQ, HEAD_SLOT))

    mods, w_in_k = _modulation(jnp.concatenate([c_ctx[None, :], c], axis=0), w_ada[lyr], b_ada[lyr],
                               jnp.swapaxes(w_in[lyr], 0, 1))

    def grp_ctx(i, tm):
        return 0

    def grp_lat(i, tm):
        return 1 + (i * tm) // l_lat

    x_ctx = x_prompt.reshape(n_ctx * l_ctx, D_MODEL)
    x_lat = x_sample.reshape(n_lat * l_lat, D_MODEL)

    x1_ctx, wg2, wu2, wd2 = _ffn(
        x_ctx, mods, grp_ctx, 0, 0, wg1, wu1, wd1, gam[0], bet[0], "ffn1_ctx",
        cast=((w_ffn_gate, (lyr, 1), CAST_ROWS_GATE_UP, 0),
              (w_ffn_up, (lyr, 1), CAST_ROWS_GATE_UP, 0),
              (w_ffn_down, (lyr, 1), CAST_ROWS_DOWN, 0)))
    x1_lat, w_out_k = _ffn(
        x_lat, mods, grp_lat, 0, 0, wg1, wu1, wd1, gam[0], bet[0], "ffn1_lat",
        cast=((w_out, (lyr,), CAST_ROWS_OUT, 0),))

    mix = (w_in_k, g_q_k, w_uq_k, g_kv_k, w_ukv_k, w_pool_k, ps, w_out_k, gam[1], bet[1])
    x2_ctx, ckv_ctx, krope_ctx = _mixer(x1_ctx, mods, grp_ctx, l_ctx, n_ctx, mix + (mq_ctx,), "ctx")
    x2_lat, _, _ = _mixer(x1_lat, mods, grp_lat, l_lat, n_lat, mix + (mq_lat,), "lat",
                          rope=(ck, sk), cache=(cache_ckv, cache_krope))

    wg2, wu2, wd2 = wg2[None], wu2[None], wd2[None]
    y_ctx, = _ffn(x2_ctx, mods, grp_ctx, 6, 0, wg2, wu2, wd2, gam[2], bet[2], "ffn2_ctx")
    y_lat, = _ffn(x2_lat, mods, grp_lat, 6, 0, wg2, wu2, wd2, gam[2], bet[2], "ffn2_lat")

    return (y_ctx.reshape(n_ctx, l_ctx, D_MODEL),
            y_lat.reshape(n_lat, l_lat, D_MODEL),
            ckv_ctx.reshape(n_ctx, DEPTH, l_ctx, KV_LORA_RANK),
            krope_ctx.reshape(n_ctx, DEPTH, l_ctx, QK_ROPE_DIM))
```

```python
import functools

import jax
import jax.numpy as jnp
from jax import lax
from jax.experimental import pallas as pl
from jax.experimental.pallas import tpu as pltpu

F32 = jnp.float32
BF16 = jnp.bfloat16

D_MODEL = 4096
N_HEADS = 16
QK_NOPE_DIM = 128
QK_ROPE_DIM = 64
V_HEAD_DIM = 128
Q_LORA_RANK = 1024
KV_LORA_RANK = 512
POOL_WIDTH = 2048
POOL_WINDOWS = (2, 4, 8, 16)
POOL_GROUP_DIM = POOL_WIDTH // len(POOL_WINDOWS)
D_FF = 11008
GRID_W = 64
ROPE_THETA = 10000.0
LN_EPS = 1e-5
RMS_EPS = 1e-6
N_MOD = 9
DEPTH = 1
DEEPNORM_ALPHA = (2.0 * DEPTH) ** 0.25
ATTN_SCALE = (QK_NOPE_DIM + QK_ROPE_DIM) ** -0.5

LANES = 128
VMEM_MIB = 1024 * 1024

HEAD_SLOT = 256
FFN_TM = 512
FFN_TF = 512
FFN_PIECE = 32
FFN_DOWN_TN = 1024
CAST_ROWS_GATE_UP = 16
CAST_ROWS_DOWN = 32
CAST_ROWS_OUT = 32
CAST_CHUNK = 16
LN_ROWS = 32
IN_N = 3840
IN_TM = 1024
IN_TN = 768
IN_PIECE = 256
OUT_TM = 512
OUT_TN = 1024
OUT_PIECE = 128
MOD_TN = 512
MOD_ROWS = 16
MOD_WIN_ROWS = 64
MOD_WUQ_ROWS = 32
POOL_PAD = 8
ATT_TQ = 256
KV_ROWS = 256


def _cparams(sem, vmem_mib):
    return pltpu.CompilerParams(dimension_semantics=sem, vmem_limit_bytes=vmem_mib * VMEM_MIB)


def _sigmoid(x):
    return 1.0 / (1.0 + jnp.exp(-x))


def _ln(x):
    mu = jnp.mean(x, axis=-1, keepdims=True)
    xc = x - mu
    var = jnp.mean(xc * xc, axis=-1, keepdims=True)
    return xc * lax.rsqrt(var + LN_EPS)


def _and(*conds):
    out = conds[0]
    for c in conds[1:]:
        out = jnp.logical_and(out, c)
    return out


def _mod_kernel(c_ref, w_ref, b_ref, wint_ref, wuq_ref, o_ref, winkt_ref, wuqk_ref, s_ref, *,
                n_src_blocks, n_uq_blocks):
    n = pl.program_id(0)

    @pl.when(n == 0)
    def _():
        c = c_ref[...]
        s_ref[...] = (c * _sigmoid(c)).astype(BF16)

    o_ref[...] = jnp.dot(s_ref[...], w_ref[...].astype(BF16), preferred_element_type=F32) + b_ref[...]

    @pl.when(n < n_src_blocks)
    def _():
        for s0 in range(0, wint_ref.shape[0], CAST_CHUNK):
            winkt_ref[s0:s0 + CAST_CHUNK, :] = wint_ref[s0:s0 + CAST_CHUNK, :].astype(BF16)

    @pl.when(n >= n_src_blocks)
    def _():
        winkt_ref[...] = jnp.zeros(winkt_ref.shape, BF16)

    @pl.when(n < n_uq_blocks)
    def _():
        src = wuq_ref[...]
        head = QK_NOPE_DIM + QK_ROPE_DIM
        half = QK_ROPE_DIM // 2
        parts = []
        for h in range(N_HEADS):
            b = h * head
            parts += [src[:, b:b + head],
                      src[:, b + QK_NOPE_DIM + half:b + head],
                      src[:, b + QK_NOPE_DIM:b + QK_NOPE_DIM + half]]
        wuqk_ref[...] = jnp.concatenate(parts, axis=1).astype(BF16)


def _modulation(cond, w_ada, b_ada, w_in_t, w_uq):
    n_cond = cond.shape[0]
    n_out = w_ada.shape[1]
    n_steps = n_out // MOD_TN
    n_src = w_in_t.shape[0] // MOD_WIN_ROWS
    n_rest = (Q_LORA_RANK + KV_LORA_RANK + QK_ROPE_DIM) // MOD_WIN_ROWS
    n_pool = POOL_WIDTH // MOD_WIN_ROWS
    n_dst = IN_N // MOD_WIN_ROWS
    assert w_in_t.shape[0] == (n_rest + n_pool) * MOD_WIN_ROWS and n_dst <= n_steps

    def dst_blk(n):
        return jnp.where(n < n_rest, n_pool + n, jnp.where(n < n_src, n - n_rest, jnp.minimum(n, n_dst - 1)))

    n_uq = w_uq.shape[0] // MOD_WUQ_ROWS
    assert n_uq <= n_steps

    cond_rows = jnp.pad(cond, ((0, MOD_ROWS - n_cond), (0, 0)))
    out, w_in_kt, w_uq_k = pl.pallas_call(
        functools.partial(_mod_kernel, n_src_blocks=n_src, n_uq_blocks=n_uq),
        grid=(n_steps,),
        in_specs=[
            pl.BlockSpec((MOD_ROWS, D_MODEL), lambda n: (0, 0)),
            pl.BlockSpec((D_MODEL, MOD_TN), lambda n: (0, n)),
            pl.BlockSpec((1, MOD_TN), lambda n: (0, n)),
            pl.BlockSpec((MOD_WIN_ROWS, D_MODEL), lambda n: (jnp.minimum(n, n_src - 1), 0)),
            pl.BlockSpec((MOD_WUQ_ROWS, w_uq.shape[1]), lambda n: (jnp.minimum(n, n_uq - 1), 0)),
        ],
        out_specs=[
            pl.BlockSpec((MOD_ROWS, MOD_TN), lambda n: (0, n)),
            pl.BlockSpec((MOD_WIN_ROWS, D_MODEL), lambda n: (dst_blk(n), 0)),
            pl.BlockSpec((MOD_WUQ_ROWS, N_HEADS * HEAD_SLOT), lambda n: (jnp.minimum(n, n_uq - 1), 0)),
        ],
        out_shape=[
            jax.ShapeDtypeStruct((MOD_ROWS, n_out), F32),
            jax.ShapeDtypeStruct((IN_N, D_MODEL), BF16),
            jax.ShapeDtypeStruct((w_uq.shape[0], N_HEADS * HEAD_SLOT), BF16),
        ],
        scratch_shapes=[pltpu.VMEM((MOD_ROWS, D_MODEL), BF16)],
        compiler_params=_cparams(("arbitrary",), 40),
        name="modulation",
    )(cond_rows, w_ada, b_ada.reshape(1, n_out), w_in_t, w_uq)
    return out[:n_cond].reshape(n_cond, N_MOD, D_MODEL), w_in_kt, w_uq_k


def _ffn_kernel(*refs, k0, nj, tail, n_tiles, n_cast):
    xn_ref, xp_ref, modn_ref, modp_ref, wg_ref, wu_ref, wd_ref, gam_ref, bet_ref = refs[:9]
    cast_src = refs[9:9 + n_cast]
    o_ref = refs[9 + n_cast]
    cast_dst = refs[10 + n_cast:10 + 2 * n_cast]
    u_cur, u_nxt, acc, acc_done = refs[10 + 2 * n_cast:]
    r = pl.program_id(0)
    j = pl.program_id(1)
    n_pieces = FFN_TM // FFN_PIECE

    def cast_blocks():
        for src, dst in zip(cast_src, cast_dst):
            for s0 in range(0, src.shape[0], CAST_CHUNK):
                dst[s0:s0 + CAST_CHUNK, :] = src[s0:s0 + CAST_CHUNK, :].astype(BF16)

    @pl.when(_and(r == 0, j == 0))
    def _():
        acc[...] = jnp.zeros(acc.shape, F32)
        acc_done[...] = jnp.zeros(acc_done.shape, F32)

    def pieces():
        rows = pl.ds(pl.multiple_of(j * FFN_PIECE, FFN_PIECE), FFN_PIECE)
        shift = modn_ref[0, k0:k0 + 1, :]
        scale1 = 1.0 + modn_ref[0, k0 + 1:k0 + 2, :]
        u_nxt[rows, :] = (_ln(xn_ref[...]) * scale1 + shift).astype(BF16)
        half_gate = 0.5 * modp_ref[0, k0 + 2:k0 + 3, :]
        t = DEEPNORM_ALPHA * xp_ref[...] + half_gate * acc_done[rows, :]
        o_ref[...] = _ln(t) * gam_ref[...] + bet_ref[...]

    def matmuls(cols):
        u = u_cur[...]
        g = jnp.dot(u, wg_ref[:, :cols], preferred_element_type=F32)
        up = jnp.dot(u, wu_ref[:, :cols], preferred_element_type=F32)
        h = (g * _sigmoid(g) * up).astype(BF16)
        for c0 in range(0, D_MODEL, FFN_DOWN_TN):
            acc[:, c0:c0 + FFN_DOWN_TN] += jnp.dot(
                h, wd_ref[:cols, c0:c0 + FFN_DOWN_TN], preferred_element_type=F32)

    def hand_over():
        acc_done[...] = acc[...]
        acc[...] = jnp.zeros(acc.shape, F32)
        u_cur[...] = u_nxt[...]

    main = _and(r >= 1, r <= n_tiles)
    edge = jnp.logical_not(main)

    @pl.when(_and(main, j < n_pieces))
    def _():
        pieces()
        cast_blocks()
        matmuls(wg_ref.shape[1])

    @pl.when(_and(main, j >= n_pieces, j < nj - 1))
    def _():
        cast_blocks()
        matmuls(wg_ref.shape[1])

    @pl.when(_and(main, j == nj - 1))
    def _():
        cast_blocks()
        matmuls(tail)
        hand_over()

    @pl.when(_and(edge, j < n_pieces))
    def _():
        pieces()
        cast_blocks()

    if n_cast:
        @pl.when(_and(edge, j >= n_pieces, j < nj - 1))
        def _():
            cast_blocks()

    @pl.when(_and(edge, j == nj - 1))
    def _():
        cast_blocks()
        hand_over()


def _ffn(x, mods, group_of_tile, k0, which, wg, wu, wd, gam, bet, name, cast=(), tf=FFN_TF):
    t = x.shape[0]
    n_tiles = t // FFN_TM
    n_pieces = FFN_TM // FFN_PIECE
    nj = pl.cdiv(D_FF, tf)
    assert n_pieces <= nj - 1
    n_steps = (n_tiles + 2) * nj

    cast_in_specs, cast_out_specs, cast_out_shapes, cast_args = [], [], [], []
    for src, lead, rows, first in cast:
        n_rows, n_cols = src.shape[-2:]
        n_blocks = n_rows // rows
        assert n_rows % rows == 0 and first + n_blocks <= n_steps

        def blk(r, j, first=first, n_blocks=n_blocks):
            return jnp.clip(r * nj + j - first, 0, n_blocks - 1)

        cast_in_specs.append(pl.BlockSpec(
            (None,) * len(lead) + (rows, n_cols), lambda r, j, lead=lead, blk=blk: lead + (blk(r, j), 0)))
        cast_out_specs.append(pl.BlockSpec((rows, n_cols), lambda r, j, blk=blk: (blk(r, j), 0)))
        cast_out_shapes.append(jax.ShapeDtypeStruct((n_rows, n_cols), BF16))
        cast_args.append(src)

    def t_next(r):
        return jnp.minimum(r, n_tiles - 1)

    def t_prev(r):
        return jnp.clip(r - 2, 0, n_tiles - 1)

    def piece(j):
        return jnp.minimum(j, n_pieces - 1)

    def w_blk(r, j):
        return jnp.where(r == 0, 0, jnp.where(r == n_tiles + 1, nj - 1, j))

    outs = pl.pallas_call(
        functools.partial(_ffn_kernel, k0=k0, nj=nj, tail=D_FF - (nj - 1) * tf, n_tiles=n_tiles,
                          n_cast=len(cast)),
        grid=(n_tiles + 2, nj),
        in_specs=[
            pl.BlockSpec((FFN_PIECE, D_MODEL), lambda r, j: (t_next(r) * n_pieces + piece(j), 0)),
            pl.BlockSpec((FFN_PIECE, D_MODEL), lambda r, j: (t_prev(r) * n_pieces + piece(j), 0)),
            pl.BlockSpec((1, N_MOD, D_MODEL), lambda r, j: (group_of_tile(t_next(r), FFN_TM), 0, 0)),
            pl.BlockSpec((1, N_MOD, D_MODEL), lambda r, j: (group_of_tile(t_prev(r), FFN_TM), 0, 0)),
            pl.BlockSpec((None, D_MODEL, tf), lambda r, j: (which, 0, w_blk(r, j))),
            pl.BlockSpec((None, D_MODEL, tf), lambda r, j: (which, 0, w_blk(r, j))),
            pl.BlockSpec((None, tf, D_MODEL), lambda r, j: (which, w_blk(r, j), 0)),
            pl.BlockSpec((1, D_MODEL), lambda r, j: (0, 0)),
            pl.BlockSpec((1, D_MODEL), lambda r, j: (0, 0)),
        ] + cast_in_specs,
        out_specs=[pl.BlockSpec((FFN_PIECE, D_MODEL),
                                lambda r, j: (jnp.where(r < 2, 0, t_prev(r) * n_pieces + piece(j)), 0))]
        + cast_out_specs,
        out_shape=[jax.ShapeDtypeStruct((t, D_MODEL), F32)] + cast_out_shapes,
        scratch_shapes=[
            pltpu.VMEM((FFN_TM, D_MODEL), BF16),
            pltpu.VMEM((FFN_TM, D_MODEL), BF16),
            pltpu.VMEM((FFN_TM, D_MODEL), F32),
            pltpu.VMEM((FFN_TM, D_MODEL), F32),
        ],
        compiler_params=_cparams(("arbitrary", "arbitrary"), 62),
        name=name,
    )(x, x, mods, mods, wg, wu, wd, gam, bet, *cast_args)
    return outs


def _inproj_kernel(xn_ref, modn_ref, w_ref, o_ref, u_cur, u_nxt, *, k0, nj):
    r = pl.program_id(0)
    j = pl.program_id(1)
    n_pieces = IN_TM // IN_PIECE

    def pieces():
        shift = modn_ref[0, k0:k0 + 1, :]
        scale1 = 1.0 + modn_ref[0, k0 + 1:k0 + 2, :]
        for s0 in range(0, IN_PIECE, LN_ROWS):
            rows = pl.ds(pl.multiple_of(j * IN_PIECE + s0, LN_ROWS), LN_ROWS)
            u_nxt[rows, :] = (_ln(xn_ref[s0:s0 + LN_ROWS, :]) * scale1 + shift).astype(BF16)

    def matmul():
        o_ref[...] = lax.dot_general(u_cur[...], w_ref[...], (((1,), (1,)), ((), ())),
                                     preferred_element_type=F32)

    def hand_over():
        u_cur[...] = u_nxt[...]

    @pl.when(_and(r >= 1, j < n_pieces))
    def _():
        pieces()
        matmul()

    @pl.when(_and(r >= 1, j >= n_pieces, j < nj - 1))
    def _():
        matmul()

    @pl.when(_and(r >= 1, j == nj - 1))
    def _():
        matmul()
        hand_over()

    @pl.when(_and(r == 0, j < n_pieces))
    def _():
        pieces()

    @pl.when(_and(r == 0, j == nj - 1))
    def _():
        hand_over()


def _inproj(x, mods, group_of_tile, k0, w_in, name):
    t = x.shape[0]
    n_tiles = t // IN_TM
    n_pieces = IN_TM // IN_PIECE
    nj = IN_N // IN_TN
    assert n_pieces <= nj - 1

    def t_next(r):
        return jnp.minimum(r, n_tiles - 1)

    def piece(j):
        return jnp.minimum(j, n_pieces - 1)

    def col_blk(r, j):
        return jnp.where(r == 0, 0, j)

    return pl.pallas_call(
        functools.partial(_inproj_kernel, k0=k0, nj=nj),
        grid=(n_tiles + 1, nj),
        in_specs=[
            pl.BlockSpec((IN_PIECE, D_MODEL), lambda r, j: (t_next(r) * n_pieces + piece(j), 0)),
            pl.BlockSpec((1, N_MOD, D_MODEL), lambda r, j: (group_of_tile(t_next(r), IN_TM), 0, 0)),
            pl.BlockSpec((IN_TN, D_MODEL), lambda r, j: (col_blk(r, j), 0)),
        ],
        out_specs=pl.BlockSpec((IN_TM, IN_TN), lambda r, j: (jnp.maximum(r - 1, 0), col_blk(r, j))),
        out_shape=jax.ShapeDtypeStruct((t, IN_N), F32),
        scratch_shapes=[
            pltpu.VMEM((IN_TM, D_MODEL), BF16),
            pltpu.VMEM((IN_TM, D_MODEL), BF16),
        ],
        compiler_params=_cparams(("arbitrary", "arbitrary"), 56),
        name=name,
    )(x, mods, w_in)


def _norms_kernel(cq_ref, ckv_ref, kr_ref, gq_ref, gkv_ref, qin_ref, ckvn_ref, krope_ref):
    def rms(x, g):
        return x * lax.rsqrt(jnp.mean(x * x, axis=-1, keepdims=True) + RMS_EPS) * g
    qin_ref[...] = rms(cq_ref[...], gq_ref[...]).astype(BF16)
    ckvn_ref[...] = rms(ckv_ref[...], gkv_ref[...])
    krope_ref[...] = kr_ref[:, :QK_ROPE_DIM]


def _norms(h, g_q, g_kv, name):
    t = h.shape[0]
    tm = 512
    kr_w = 256
    return pl.pallas_call(
        _norms_kernel,
        grid=(t // tm,),
        in_specs=[
            pl.BlockSpec((tm, Q_LORA_RANK), lambda i: (i, POOL_WIDTH // Q_LORA_RANK)),
            pl.BlockSpec((tm, KV_LORA_RANK), lambda i: (i, (POOL_WIDTH + Q_LORA_RANK) // KV_LORA_RANK)),
            pl.BlockSpec((tm, kr_w), lambda i: (i, (POOL_WIDTH + Q_LORA_RANK + KV_LORA_RANK) // kr_w)),
            pl.BlockSpec((1, Q_LORA_RANK), lambda i: (0, 0)),
            pl.BlockSpec((1, KV_LORA_RANK), lambda i: (0, 0)),
        ],
        out_specs=[
            pl.BlockSpec((tm, Q_LORA_RANK), lambda i: (i, 0)),
            pl.BlockSpec((tm, KV_LORA_RANK), lambda i: (i, 0)),
            pl.BlockSpec((tm, QK_ROPE_DIM), lambda i: (i, 0)),
        ],
        out_shape=[
            jax.ShapeDtypeStruct((t, Q_LORA_RANK), BF16),
            jax.ShapeDtypeStruct((t, KV_LORA_RANK), F32),
            jax.ShapeDtypeStruct((t, QK_ROPE_DIM), F32),
        ],
        compiler_params=_cparams(("arbitrary",), 32),
        name=name,
    )(h, h, h, g_q, g_kv)


def _pool_kernel(xp_ref, w_ref, ps_ref, o_ref, pad_ref):
    seq = xp_ref.shape[0]
    zeros = jnp.zeros((POOL_PAD, POOL_WIDTH), F32)
    pad_ref[0:POOL_PAD, :] = zeros
    pad_ref[POOL_PAD + seq:POOL_PAD + seq + POOL_PAD, :] = zeros
    pad_ref[POOL_PAD:POOL_PAD + seq, :] = xp_ref[...]
    pos = lax.broadcasted_iota(jnp.int32, (seq, 1), 0)
    for g, win in enumerate(POOL_WINDOWS):
        half = win // 2
        c0 = g * POOL_GROUP_DIM
        acc = pad_ref[POOL_PAD - half:POOL_PAD - half + seq, c0:c0 + POOL_GROUP_DIM]
        for d in range(-half + 1, half):
            acc = acc + pad_ref[POOL_PAD + d:POOL_PAD + d + seq, c0:c0 + POOL_GROUP_DIM]
        count = (jnp.minimum(pos + half, seq) - jnp.maximum(pos - half, 0)).astype(F32)
        pooled = acc / count - xp_ref[:, c0:c0 + POOL_GROUP_DIM]
        mixed = jnp.dot(pooled.astype(BF16), w_ref[g], preferred_element_type=F32)
        o_ref[:, c0:c0 + POOL_GROUP_DIM] = (mixed * ps_ref[:, c0:c0 + POOL_GROUP_DIM]).astype(BF16)


def _pool(h, seq, w_pool, pool_scale, name):
    t = h.shape[0]
    return pl.pallas_call(
        _pool_kernel,
        grid=(t // seq,),
        in_specs=[
            pl.BlockSpec((seq, POOL_WIDTH), lambda i: (i, 0)),
            pl.BlockSpec(w_pool.shape, lambda i: (0, 0, 0)),
            pl.BlockSpec((1, POOL_WIDTH), lambda i: (0, 0)),
        ],
        out_specs=pl.BlockSpec((seq, POOL_WIDTH), lambda i: (i, 0)),
        out_shape=jax.ShapeDtypeStruct((t, POOL_WIDTH), BF16),
        scratch_shapes=[pltpu.VMEM((seq + 2 * POOL_PAD, POOL_WIDTH), F32)],
        compiler_params=_cparams(("arbitrary",), 56),
        name=name,
    )(h, w_pool, pool_scale)


def _attn_kernel(*refs, la, lb):
    if lb:
        (qin_ref, mq_ref, ckva_ref, kra_ref, ck_ref, sk_ref, ckvb_ref, krb_ref,
         wuq_ref, wukv_ref, o_ref, kv_ref, krd_ref) = refs
    else:
        qin_ref, mq_ref, ckva_ref, kra_ref, wuq_ref, wukv_ref, o_ref, kv_ref, krd_ref = refs

    @pl.when(pl.program_id(1) == 0)
    def _():
        wukv = wukv_ref[...]
        for r0 in range(0, la, KV_ROWS):
            kv_ref[r0:r0 + KV_ROWS, :] = jnp.dot(
                ckva_ref[r0:r0 + KV_ROWS, :].astype(BF16), wukv, preferred_element_type=F32).astype(BF16)
        kr = kra_ref[...]
        if lb:
            kr2 = jnp.concatenate([kr, kr], axis=1)
            kr2 = kr2 * ck_ref[...] + pltpu.roll(kr2, QK_ROPE_DIM // 2, axis=1) * sk_ref[...]
            krd_ref[0:la, :] = kr2.astype(BF16)
            kv_ref[la:la + lb, :] = jnp.dot(
                ckvb_ref[0, 0].astype(BF16), wukv, preferred_element_type=F32).astype(BF16)
            krb = krb_ref[0, 0]
            krd_ref[la:la + lb, :] = jnp.concatenate([krb, krb], axis=1).astype(BF16)
        else:
            krd_ref[...] = jnp.concatenate([kr, jnp.zeros_like(kr)], axis=1).astype(BF16)

    q = jnp.dot(qin_ref[...], wuq_ref[...], preferred_element_type=F32)
    mq = mq_ref[...]
    krd = krd_ref[...]
    for h in range(N_HEADS):
        c0 = h * HEAD_SLOT
        qh = (q[:, c0:c0 + HEAD_SLOT] * mq).astype(BF16)
        kh = jnp.concatenate([kv_ref[:, c0:c0 + QK_NOPE_DIM], krd], axis=1)
        s = lax.dot_general(qh, kh, (((1,), (1,)), ((), ())), preferred_element_type=F32)
        e = jnp.exp(s - jnp.max(s, axis=-1, keepdims=True))
        denom = jnp.sum(e, axis=-1, keepdims=True)
        o = jnp.dot(e.astype(BF16), kv_ref[:, c0 + QK_NOPE_DIM:c0 + HEAD_SLOT], preferred_element_type=F32)
        o_ref[:, h * V_HEAD_DIM:(h + 1) * V_HEAD_DIM] = (o / denom).astype(BF16)


def _attention(q_in, mq, ckv, krope, w_uq, w_ukv, n_batch, la, name, rope=None, cache=None):
    nq = la // ATT_TQ
    lb = 0 if cache is None else cache[0].shape[2]
    mq_rows = mq.shape[0]
    in_specs = [
        pl.BlockSpec((ATT_TQ, Q_LORA_RANK), lambda b, qi: (b * nq + qi, 0)),
        pl.BlockSpec((ATT_TQ, HEAD_SLOT), (lambda b, qi: (qi, 0)) if mq_rows > ATT_TQ else (lambda b, qi: (0, 0))),
        pl.BlockSpec((la, KV_LORA_RANK), lambda b, qi: (b, 0)),
        pl.BlockSpec((la, QK_ROPE_DIM), lambda b, qi: (b, 0)),
    ]
    args = [q_in, mq, ckv, krope]
    if lb:
        ck, sk = rope
        cache_ckv, cache_krope = cache
        in_specs += [
            pl.BlockSpec((la, LANES), lambda b, qi: (0, 0)),
            pl.BlockSpec((la, LANES), lambda b, qi: (0, 0)),
            pl.BlockSpec((1, 1, lb, KV_LORA_RANK), lambda b, qi: (b, 0, 0, 0)),
            pl.BlockSpec((1, 1, lb, QK_ROPE_DIM), lambda b, qi: (b, 0, 0, 0)),
        ]
        args += [ck, sk, cache_ckv, cache_krope]
    in_specs += [
        pl.BlockSpec(w_uq.shape, lambda b, qi: (0, 0), pipeline_mode=pl.Buffered(1)),
        pl.BlockSpec(w_ukv.shape, lambda b, qi: (0, 0), pipeline_mode=pl.Buffered(1)),
    ]
    args += [w_uq, w_ukv]
    return pl.pallas_call(
        functools.partial(_attn_kernel, la=la, lb=lb),
        grid=(n_batch, nq),
        in_specs=in_specs,
        out_specs=pl.BlockSpec((ATT_TQ, N_HEADS * V_HEAD_DIM), lambda b, qi: (b * nq + qi, 0)),
        out_shape=jax.ShapeDtypeStruct((n_batch * la, N_HEADS * V_HEAD_DIM), BF16),
        scratch_shapes=[
            pltpu.VMEM((la + lb, N_HEADS * HEAD_SLOT), BF16),
            pltpu.VMEM((la + lb, LANES), BF16),
        ],
        compiler_params=_cparams(("arbitrary", "arbitrary"), 56),
        name=name,
    )(*args)


def _outproj_kernel(a_ref, p_ref, wa_ref, wp_ref, xp_ref, modp_ref, gam_ref, bet_ref, o_ref,
                    acc, acc_done, *, k0, nj, n_tiles):
    r = pl.program_id(0)
    j = pl.program_id(1)
    tn = wa_ref.shape[1]

    @pl.when(_and(r == 0, j == 0))
    def _():
        acc_done[...] = jnp.zeros(acc_done.shape, F32)

    def pieces():
        for s0 in range(0, OUT_PIECE, LN_ROWS):
            rows = pl.ds(pl.multiple_of(j * OUT_PIECE + s0, LN_ROWS), LN_ROWS)
            parts = []
            for jj in range(nj):
                cols = slice(jj * tn, (jj + 1) * tn)
                gate = modp_ref[0, k0 + 2:k0 + 3, cols]
                parts.append(DEEPNORM_ALPHA * xp_ref[s0:s0 + LN_ROWS, cols] + gate * acc_done[jj, rows, :])
            mu = sum(jnp.sum(p, axis=-1, keepdims=True) for p in parts) * (1.0 / D_MODEL)
            cen = [p - mu for p in parts]
            var = sum(jnp.sum(c * c, axis=-1, keepdims=True) for c in cen) * (1.0 / D_MODEL)
            rstd = lax.rsqrt(var + LN_EPS)
            for jj in range(nj):
                cols = slice(jj * tn, (jj + 1) * tn)
                o_ref[s0:s0 + LN_ROWS, cols] = cen[jj] * rstd * gam_ref[:, cols] + bet_ref[:, cols]

    def matmul():
        acc[j] = (jnp.dot(a_ref[...], wa_ref[...], preferred_element_type=F32)
                  + jnp.dot(p_ref[...], wp_ref[...], preferred_element_type=F32))

    def hand_over():
        acc_done[...] = acc[...]

    @pl.when(_and(r < n_tiles, j < nj - 1))
    def _():
        pieces()
        matmul()

    @pl.when(_and(r < n_tiles, j == nj - 1))
    def _():
        pieces()
        matmul()
        hand_over()

    @pl.when(r == n_tiles)
    def _():
        pieces()


def _outproj(attn, pool, w_out, x, mods, group_of_tile, k0, gam, bet, name):
    t = x.shape[0]
    n_tiles = t // OUT_TM
    nj = D_MODEL // OUT_TN
    assert OUT_TM // OUT_PIECE == nj
    half = attn.shape[1]

    def t_cur(r):
        return jnp.minimum(r, n_tiles - 1)

    def t_prev(r):
        return jnp.maximum(r - 1, 0)

    def w_blk(r, j):
        return jnp.where(r == n_tiles, nj - 1, j)

    return pl.pallas_call(
        functools.partial(_outproj_kernel, k0=k0, nj=nj, n_tiles=n_tiles),
        grid=(n_tiles + 1, nj),
        in_specs=[
            pl.BlockSpec((OUT_TM, half), lambda r, j: (t_cur(r), 0)),
            pl.BlockSpec((OUT_TM, half), lambda r, j: (t_cur(r), 0)),
            pl.BlockSpec((half, OUT_TN), lambda r, j: (0, w_blk(r, j))),
            pl.BlockSpec((half, OUT_TN), lambda r, j: (1, w_blk(r, j))),
            pl.BlockSpec((OUT_PIECE, D_MODEL), lambda r, j: (t_prev(r) * nj + j, 0)),
            pl.BlockSpec((1, N_MOD, D_MODEL), lambda r, j: (group_of_tile(t_prev(r), OUT_TM), 0, 0)),
            pl.BlockSpec((1, D_MODEL), lambda r, j: (0, 0)),
            pl.BlockSpec((1, D_MODEL), lambda r, j: (0, 0)),
        ],
        out_specs=pl.BlockSpec((OUT_PIECE, D_MODEL), lambda r, j: (jnp.where(r == 0, 0, t_prev(r) * nj + j), 0)),
        out_shape=jax.ShapeDtypeStruct((t, D_MODEL), F32),
        scratch_shapes=[
            pltpu.VMEM((nj, OUT_TM, OUT_TN), F32),
            pltpu.VMEM((nj, OUT_TM, OUT_TN), F32),
        ],
        compiler_params=_cparams(("arbitrary", "arbitrary"), 60),
        name=name,
    )(attn, pool, w_out, w_out, x, mods, gam, bet)


def _rope_tables(n_tokens):
    rows = n_tokens // GRID_W
    row_id = jnp.repeat(jnp.arange(rows, dtype=F32), GRID_W)
    col_id = jnp.tile(jnp.arange(GRID_W, dtype=F32), rows)
    n_freq = QK_ROPE_DIM // 4
    inv_freq = ROPE_THETA ** (-jnp.arange(n_freq, dtype=F32) / n_freq)
    ang = jnp.concatenate([row_id[:, None] * inv_freq, col_id[:, None] * inv_freq], -1)
    cos, sin = jnp.cos(ang), jnp.sin(ang)
    cos2 = jnp.concatenate([cos, cos], -1)
    sin2 = jnp.concatenate([-sin, sin], -1)
    ones = jnp.ones((n_tokens, QK_NOPE_DIM), F32)
    mq = ATTN_SCALE * jnp.concatenate([ones, cos2, sin2], -1)
    ck = jnp.concatenate([cos2, cos2], -1)
    sk = jnp.concatenate([sin2, sin2], -1)
    return mq, ck, sk


def _mixer(x1, mods, group_of_tile, seq, n_batch, wts, tag, rope=None, cache=None):
    (w_in, g_q, w_uq, g_kv, w_ukv, w_pool, pool_scale, w_out, gam, bet, mq) = wts
    h = _inproj(x1, mods, group_of_tile, 3, w_in, "inproj_" + tag)
    q_in, ckv, krope = _norms(h, g_q, g_kv, "norms_" + tag)
    pool = _pool(h, seq, w_pool, pool_scale, "pool_" + tag)
    attn = _attention(q_in, mq, ckv, krope, w_uq, w_ukv, n_batch, seq, "attn_" + tag, rope=rope, cache=cache)
    x2 = _outproj(attn, pool, w_out, x1, mods, group_of_tile, 3, gam, bet, "outproj_" + tag)
    return x2, ckv, krope


def kernel(x_prompt, x_sample, cache_ckv, cache_krope, c, c_ctx, w_ada, b_ada, w_in, g_q, w_uq, g_kv, w_ukv,
           w_pool, pool_scale, w_out, w_ffn_gate, w_ffn_up, w_ffn_down, ln_gamma, ln_beta):
    n_ctx, l_ctx, _ = x_prompt.shape
    n_lat, l_lat, _ = x_sample.shape
    assert w_ada.shape[0] == DEPTH
    lyr = 0

    wg1 = w_ffn_gate[lyr, 0:1].astype(BF16)
    wu1 = w_ffn_up[lyr, 0:1].astype(BF16)
    wd1 = w_ffn_down[lyr, 0:1].astype(BF16)
    w_ukv_k = w_ukv[lyr].astype(BF16)
    w_pool_k = w_pool[lyr].astype(BF16)
    gam =[ln_gamma[lyr, k].reshape(1, D_MODEL) for k in range(3)]
    bet = [ln_beta[lyr, k].reshape(1, D_MODEL) for k in range(3)]
    g_q_k = g_q[lyr].reshape(1, Q_LORA_RANK)
    g_kv_k = g_kv[lyr].reshape(1, KV_LORA_RANK)
    ps = pool_scale[lyr].reshape(1, POOL_WIDTH)

    mq_lat, ck, sk = _rope_tables(l_lat)
    mq_ctx = jnp.broadcast_to(
        ATTN_SCALE * jnp.concatenate([jnp.ones((QK_NOPE_DIM + QK_ROPE_DIM,), F32), jnp.zeros((QK_ROPE_DIM,), F32)]),
        (ATT_TQ, HEAD_SLOT))

    mods, w_in_k, w_uq_k = _modulation(jnp.concatenate([c_ctx[None, :], c], axis=0), w_ada[lyr], b_ada[lyr],
                                       jnp.swapaxes(w_in[lyr], 0, 1), w_uq[lyr])

    def grp_ctx(i, tm):
        return 0

    def grp_lat(i, tm):
        return 1 + (i * tm) // l_lat

    x_ctx = x_prompt.reshape(n_ctx * l_ctx, D_MODEL)
    x_lat = x_sample.reshape(n_lat * l_lat, D_MODEL)

    x1_ctx, wg2, wu2, wd2 = _ffn(
        x_ctx, mods, grp_ctx, 0, 0, wg1, wu1, wd1, gam[0], bet[0], "ffn1_ctx",
        cast=((w_ffn_gate, (lyr, 1), CAST_ROWS_GATE_UP, 0),
              (w_ffn_up, (lyr, 1), CAST_ROWS_GATE_UP, 0),
              (w_ffn_down, (lyr, 1), CAST_ROWS_DOWN, 0)))
    x1_lat, w_out_k = _ffn(
        x_lat, mods, grp_lat, 0, 0, wg1, wu1, wd1, gam[0], bet[0], "ffn1_lat",
        cast=((w_out, (lyr,), CAST_ROWS_OUT, 0),))

    mix = (w_in_k, g_q_k, w_uq_k, g_kv_k, w_ukv_k, w_pool_k, ps, w_out_k, gam[1], bet[1])
    x2_ctx, ckv_ctx, krope_ctx = _mixer(x1_ctx, mods, grp_ctx, l_ctx, n_ctx, mix + (mq_ctx,), "ctx")
    x2_lat, _, _ = _mixer(x1_lat, mods, grp_lat, l_lat, n_lat, mix + (mq_lat,), "lat",
                          rope=(ck, sk), cache=(cache_ckv, cache_krope))

    wg2, wu2, wd2 = wg2[None], wu2[None], wd2[None]
    y_ctx, = _ffn(x2_ctx, mods, grp_ctx, 6, 0, wg2, wu2, wd2, gam[2], bet[2], "ffn2_ctx")
    y_lat, = _ffn(x2_lat, mods, grp_lat, 6, 0, wg2, wu2, wd2, gam[2], bet[2], "ffn2_lat")

    return (y_ctx.reshape(n_ctx, l_ctx, D_MODEL),
            y_lat.reshape(n_lat, l_lat, D_MODEL),
            ckv_ctx.reshape(n_ctx, DEPTH, l_ctx, KV_LORA_RANK),
            krope_ctx.reshape(n_ctx, DEPTH, l_ctx, QK_ROPE_DIM))
```

```python
import functools

import jax
import jax.numpy as jnp
from jax import lax
from jax.experimental import pallas as pl
from jax.experimental.pallas import tpu as pltpu

F32 = jnp.float32
BF16 = jnp.bfloat16

D_MODEL = 4096
N_HEADS = 16
QK_NOPE_DIM = 128
QK_ROPE_DIM = 64
V_HEAD_DIM = 128
Q_LORA_RANK = 1024
KV_LORA_RANK = 512
POOL_WIDTH = 2048
POOL_WINDOWS = (2, 4, 8, 16)
POOL_GROUP_DIM = POOL_WIDTH // len(POOL_WINDOWS)
D_FF = 11008
GRID_W = 64
ROPE_THETA = 10000.0
LN_EPS = 1e-5
RMS_EPS = 1e-6
N_MOD = 9
DEPTH = 1
DEEPNORM_ALPHA = (2.0 * DEPTH) ** 0.25
ATTN_SCALE = (QK_NOPE_DIM + QK_ROPE_DIM) ** -0.5

LANES = 128
VMEM_MIB = 1024 * 1024

HEAD_SLOT = 256
FFN_TM = 512
FFN_TF = 512
FFN_PIECE = 32
FFN_DOWN_TN = 1024
CAST_ROWS_GATE_UP = 16
CAST_ROWS_DOWN = 32
CAST_ROWS_OUT = 32
CAST_CHUNK = 16
LN_ROWS = 32
IN_N = 3840
IN_TM = 1024
IN_TN = 768
IN_PIECE = 256
OUT_TM = 512
OUT_TN = 1024
OUT_PIECE = 128
MOD_TN = 512
MOD_ROWS = 16
MOD_WIN_ROWS = 64
MOD_WUQ_ROWS = 32
POOL_PAD = 8
NORM_ROWS = 512
ATT_TQ = 256
KV_ROWS = 256


def _cparams(sem, vmem_mib):
    return pltpu.CompilerParams(dimension_semantics=sem, vmem_limit_bytes=vmem_mib * VMEM_MIB)


def _sigmoid(x):
    return 1.0 / (1.0 + jnp.exp(-x))


def _ln(x):
    mu = jnp.mean(x, axis=-1, keepdims=True)
    xc = x - mu
    var = jnp.mean(xc * xc, axis=-1, keepdims=True)
    return xc * lax.rsqrt(var + LN_EPS)


def _and(*conds):
    out = conds[0]
    for c in conds[1:]:
        out = jnp.logical_and(out, c)
    return out


def _mod_kernel(c_ref, w_ref, b_ref, wint_ref, wuq_ref, o_ref, winkt_ref, wuqk_ref, s_ref, *,
                n_src_blocks, n_uq_blocks):
    n = pl.program_id(0)

    @pl.when(n == 0)
    def _():
        c = c_ref[...]
        s_ref[...] = (c * _sigmoid(c)).astype(BF16)

    o_ref[...] = jnp.dot(s_ref[...], w_ref[...].astype(BF16), preferred_element_type=F32) + b_ref[...]

    @pl.when(n < n_src_blocks)
    def _():
        for s0 in range(0, wint_ref.shape[0], CAST_CHUNK):
            winkt_ref[s0:s0 + CAST_CHUNK, :] = wint_ref[s0:s0 + CAST_CHUNK, :].astype(BF16)

    @pl.when(n >= n_src_blocks)
    def _():
        winkt_ref[...] = jnp.zeros(winkt_ref.shape, BF16)

    @pl.when(n < n_uq_blocks)
    def _():
        src = wuq_ref[...]
        head = QK_NOPE_DIM + QK_ROPE_DIM
        half = QK_ROPE_DIM // 2
        parts = []
        for h in range(N_HEADS):
            b = h * head
            parts += [src[:, b:b + head],
                      src[:, b + QK_NOPE_DIM + half:b + head],
                      src[:, b + QK_NOPE_DIM:b + QK_NOPE_DIM + half]]
        wuqk_ref[...] = jnp.concatenate(parts, axis=1).astype(BF16)


def _modulation(cond, w_ada, b_ada, w_in_t, w_uq):
    n_cond = cond.shape[0]
    n_out = w_ada.shape[1]
    n_steps = n_out // MOD_TN
    n_src = w_in_t.shape[0] // MOD_WIN_ROWS
    n_rest = (Q_LORA_RANK + KV_LORA_RANK + QK_ROPE_DIM) // MOD_WIN_ROWS
    n_pool = POOL_WIDTH // MOD_WIN_ROWS
    n_dst = IN_N // MOD_WIN_ROWS
    assert w_in_t.shape[0] == (n_rest + n_pool) * MOD_WIN_ROWS and n_dst <= n_steps

    def dst_blk(n):
        return jnp.where(n < n_rest, n_pool + n, jnp.where(n < n_src, n - n_rest, jnp.minimum(n, n_dst - 1)))

    n_uq = w_uq.shape[0] // MOD_WUQ_ROWS
    assert n_uq <= n_steps

    cond_rows = jnp.pad(cond, ((0, MOD_ROWS - n_cond), (0, 0)))
    out, w_in_kt, w_uq_k = pl.pallas_call(
        functools.partial(_mod_kernel, n_src_blocks=n_src, n_uq_blocks=n_uq),
        grid=(n_steps,),
        in_specs=[
            pl.BlockSpec((MOD_ROWS, D_MODEL), lambda n: (0, 0)),
            pl.BlockSpec((D_MODEL, MOD_TN), lambda n: (0, n)),
            pl.BlockSpec((1, MOD_TN), lambda n: (0, n)),
            pl.BlockSpec((MOD_WIN_ROWS, D_MODEL), lambda n: (jnp.minimum(n, n_src - 1), 0)),
            pl.BlockSpec((MOD_WUQ_ROWS, w_uq.shape[1]), lambda n: (jnp.minimum(n, n_uq - 1), 0)),
        ],
        out_specs=[
            pl.BlockSpec((MOD_ROWS, MOD_TN), lambda n: (0, n)),
            pl.BlockSpec((MOD_WIN_ROWS, D_MODEL), lambda n: (dst_blk(n), 0)),
            pl.BlockSpec((MOD_WUQ_ROWS, N_HEADS * HEAD_SLOT), lambda n: (jnp.minimum(n, n_uq - 1), 0)),
        ],
        out_shape=[
            jax.ShapeDtypeStruct((MOD_ROWS, n_out), F32),
            jax.ShapeDtypeStruct((IN_N, D_MODEL), BF16),
            jax.ShapeDtypeStruct((w_uq.shape[0], N_HEADS * HEAD_SLOT), BF16),
        ],
        scratch_shapes=[pltpu.VMEM((MOD_ROWS, D_MODEL), BF16)],
        compiler_params=pltpu.CompilerParams(
            dimension_semantics=("arbitrary",), vmem_limit_bytes=40 * VMEM_MIB,
            allow_input_fusion=[True, False, False, False, False]),
        name="modulation",
    )(cond_rows, w_ada, b_ada.reshape(1, n_out), w_in_t, w_uq)
    return out[:n_cond].reshape(n_cond, N_MOD, D_MODEL), w_in_kt, w_uq_k


def _ffn_kernel(*refs, k0, nj, tail, n_tiles, n_cast):
    xn_ref, xp_ref, modn_ref, modp_ref, wg_ref, wu_ref, wd_ref, gam_ref, bet_ref = refs[:9]
    cast_src = refs[9:9 + n_cast]
    o_ref = refs[9 + n_cast]
    cast_dst = refs[10 + n_cast:10 + 2 * n_cast]
    u_cur, u_nxt, acc, acc_done = refs[10 + 2 * n_cast:]
    r = pl.program_id(0)
    j = pl.program_id(1)
    n_pieces = FFN_TM // FFN_PIECE

    def cast_blocks():
        for src, dst in zip(cast_src, cast_dst):
            for s0 in range(0, src.shape[0], CAST_CHUNK):
                dst[s0:s0 + CAST_CHUNK, :] = src[s0:s0 + CAST_CHUNK, :].astype(BF16)

    @pl.when(_and(r == 0, j == 0))
    def _():
        acc[...] = jnp.zeros(acc.shape, F32)
        acc_done[...] = jnp.zeros(acc_done.shape, F32)

    def pieces():
        rows = pl.ds(pl.multiple_of(j * FFN_PIECE, FFN_PIECE), FFN_PIECE)
        shift = modn_ref[0, k0:k0 + 1, :]
        scale1 = 1.0 + modn_ref[0, k0 + 1:k0 + 2, :]
        u_nxt[rows, :] = (_ln(xn_ref[...]) * scale1 + shift).astype(BF16)
        half_gate = 0.5 * modp_ref[0, k0 + 2:k0 + 3, :]
        t = DEEPNORM_ALPHA * xp_ref[...] + half_gate * acc_done[rows, :]
        o_ref[...] = _ln(t) * gam_ref[...] + bet_ref[...]

    def matmuls(cols):
        u = u_cur[...]
        g = jnp.dot(u, wg_ref[:, :cols], preferred_element_type=F32)
        up = jnp.dot(u, wu_ref[:, :cols], preferred_element_type=F32)
        h = (g * _sigmoid(g) * up).astype(BF16)
        for c0 in range(0, D_MODEL, FFN_DOWN_TN):
            acc[:, c0:c0 + FFN_DOWN_TN] += jnp.dot(
                h, wd_ref[:cols, c0:c0 + FFN_DOWN_TN], preferred_element_type=F32)

    def hand_over():
        acc_done[...] = acc[...]
        acc[...] = jnp.zeros(acc.shape, F32)
        u_cur[...] = u_nxt[...]

    main = _and(r >= 1, r <= n_tiles)
    edge = jnp.logical_not(main)

    @pl.when(_and(main, j < n_pieces))
    def _():
        pieces()
        cast_blocks()
        matmuls(wg_ref.shape[1])

    @pl.when(_and(main, j >= n_pieces, j < nj - 1))
    def _():
        cast_blocks()
        matmuls(wg_ref.shape[1])

    @pl.when(_and(main, j == nj - 1))
    def _():
        cast_blocks()
        matmuls(tail)
        hand_over()

    @pl.when(_and(edge, j < n_pieces))
    def _():
        pieces()
        cast_blocks()

    if n_cast:
        @pl.when(_and(edge, j >= n_pieces, j < nj - 1))
        def _():
            cast_blocks()

    @pl.when(_and(edge, j == nj - 1))
    def _():
        cast_blocks()
        hand_over()


def _ffn(x, mods, group_of_tile, k0, which, wg, wu, wd, gam, bet, name, cast=(), tf=FFN_TF):
    t = x.shape[0]
    n_tiles = t // FFN_TM
    n_pieces = FFN_TM // FFN_PIECE
    nj = pl.cdiv(D_FF, tf)
    assert n_pieces <= nj - 1
    n_steps = (n_tiles + 2) * nj

    cast_in_specs, cast_out_specs, cast_out_shapes, cast_args = [], [], [], []
    for src, lead, rows, first in cast:
        n_rows, n_cols = src.shape[-2:]
        n_blocks = n_rows // rows
        assert n_rows % rows == 0 and first + n_blocks <= n_steps

        def blk(r, j, first=first, n_blocks=n_blocks):
            return jnp.clip(r * nj + j - first, 0, n_blocks - 1)

        cast_in_specs.append(pl.BlockSpec(
            (None,) * len(lead) + (rows, n_cols), lambda r, j, lead=lead, blk=blk: lead + (blk(r, j), 0)))
        cast_out_specs.append(pl.BlockSpec((rows, n_cols), lambda r, j, blk=blk: (blk(r, j), 0)))
        cast_out_shapes.append(jax.ShapeDtypeStruct((n_rows, n_cols), BF16))
        cast_args.append(src)

    def t_next(r):
        return jnp.minimum(r, n_tiles - 1)

    def t_prev(r):
        return jnp.clip(r - 2, 0, n_tiles - 1)

    def piece(j):
        return jnp.minimum(j, n_pieces - 1)

    def w_blk(r, j):
        return jnp.where(r == 0, 0, jnp.where(r == n_tiles + 1, nj - 1, j))

    outs = pl.pallas_call(
        functools.partial(_ffn_kernel, k0=k0, nj=nj, tail=D_FF - (nj - 1) * tf, n_tiles=n_tiles,
                          n_cast=len(cast)),
        grid=(n_tiles + 2, nj),
        in_specs=[
            pl.BlockSpec((FFN_PIECE, D_MODEL), lambda r, j: (t_next(r) * n_pieces + piece(j), 0)),
            pl.BlockSpec((FFN_PIECE, D_MODEL), lambda r, j: (t_prev(r) * n_pieces + piece(j), 0)),
            pl.BlockSpec((1, N_MOD, D_MODEL), lambda r, j: (group_of_tile(t_next(r), FFN_TM), 0, 0)),
            pl.BlockSpec((1, N_MOD, D_MODEL), lambda r, j: (group_of_tile(t_prev(r), FFN_TM), 0, 0)),
            pl.BlockSpec((None, D_MODEL, tf), lambda r, j: (which, 0, w_blk(r, j))),
            pl.BlockSpec((None, D_MODEL, tf), lambda r, j: (which, 0, w_blk(r, j))),
            pl.BlockSpec((None, tf, D_MODEL), lambda r, j: (which, w_blk(r, j), 0)),
            pl.BlockSpec((1, D_MODEL), lambda r, j: (0, 0)),
            pl.BlockSpec((1, D_MODEL), lambda r, j: (0, 0)),
        ] + cast_in_specs,
        out_specs=[pl.BlockSpec((FFN_PIECE, D_MODEL),
                                lambda r, j: (jnp.where(r < 2, 0, t_prev(r) * n_pieces + piece(j)), 0))]
        + cast_out_specs,
        out_shape=[jax.ShapeDtypeStruct((t, D_MODEL), F32)] + cast_out_shapes,
        scratch_shapes=[
            pltpu.VMEM((FFN_TM, D_MODEL), BF16),
            pltpu.VMEM((FFN_TM, D_MODEL), BF16),
            pltpu.VMEM((FFN_TM, D_MODEL), F32),
            pltpu.VMEM((FFN_TM, D_MODEL), F32),
        ],
        compiler_params=_cparams(("arbitrary", "arbitrary"), 62),
        name=name,
    )(x, x, mods, mods, wg, wu, wd, gam, bet, *cast_args)
    return outs


def _inproj_kernel(xn_ref, modn_ref, w_ref, o_ref, u_cur, u_nxt, *, k0, nj):
    r = pl.program_id(0)
    j = pl.program_id(1)
    n_pieces = IN_TM // IN_PIECE

    def pieces():
        shift = modn_ref[0, k0:k0 + 1, :]
        scale1 = 1.0 + modn_ref[0, k0 + 1:k0 + 2, :]
        for s0 in range(0, IN_PIECE, LN_ROWS):
            rows = pl.ds(pl.multiple_of(j * IN_PIECE + s0, LN_ROWS), LN_ROWS)
            u_nxt[rows, :] = (_ln(xn_ref[s0:s0 + LN_ROWS, :]) * scale1 + shift).astype(BF16)

    def matmul():
        o_ref[...] = lax.dot_general(u_cur[...], w_ref[...], (((1,), (1,)), ((), ())),
                                     preferred_element_type=F32)

    def hand_over():
        u_cur[...] = u_nxt[...]

    @pl.when(_and(r >= 1, j < n_pieces))
    def _():
        pieces()
        matmul()

    @pl.when(_and(r >= 1, j >= n_pieces, j < nj - 1))
    def _():
        matmul()

    @pl.when(_and(r >= 1, j == nj - 1))
    def _():
        matmul()
        hand_over()

    @pl.when(_and(r == 0, j < n_pieces))
    def _():
        pieces()

    @pl.when(_and(r == 0, j == nj - 1))
    def _():
        hand_over()


def _inproj(x, mods, group_of_tile, k0, w_in, name):
    t = x.shape[0]
    n_tiles = t // IN_TM
    n_pieces = IN_TM // IN_PIECE
    nj = IN_N // IN_TN
    assert n_pieces <= nj - 1

    def t_next(r):
        return jnp.minimum(r, n_tiles - 1)

    def piece(j):
        return jnp.minimum(j, n_pieces - 1)

    def col_blk(r, j):
        return jnp.where(r == 0, 0, j)

    return pl.pallas_call(
        functools.partial(_inproj_kernel, k0=k0, nj=nj),
        grid=(n_tiles + 1, nj),
        in_specs=[
            pl.BlockSpec((IN_PIECE, D_MODEL), lambda r, j: (t_next(r) * n_pieces + piece(j), 0)),
            pl.BlockSpec((1, N_MOD, D_MODEL), lambda r, j: (group_of_tile(t_next(r), IN_TM), 0, 0)),
            pl.BlockSpec((IN_TN, D_MODEL), lambda r, j: (col_blk(r, j), 0)),
        ],
        out_specs=pl.BlockSpec((IN_TM, IN_TN), lambda r, j: (jnp.maximum(r - 1, 0), col_blk(r, j))),
        out_shape=jax.ShapeDtypeStruct((t, IN_N), F32),
        scratch_shapes=[
            pltpu.VMEM((IN_TM, D_MODEL), BF16),
            pltpu.VMEM((IN_TM, D_MODEL), BF16),
        ],
        compiler_params=_cparams(("arbitrary", "arbitrary"), 56),
        name=name,
    )(x, mods, w_in)


def _pool_norms_kernel(xp_ref, w_ref, ps_ref, cq_ref, ckv_ref, kr_ref, gq_ref, gkv_ref,
                       o_ref, qin_ref, ckvn_ref, krope_ref, pad_ref):
    def rms(x, g):
        return x * lax.rsqrt(jnp.mean(x * x, axis=-1, keepdims=True) + RMS_EPS) * g
    qin_ref[...] = rms(cq_ref[...], gq_ref[...]).astype(BF16)
    ckvn_ref[...] = rms(ckv_ref[...], gkv_ref[...])
    krope_ref[...] = kr_ref[:, :QK_ROPE_DIM]

    @pl.when(pl.program_id(1) == 0)
    def _():
        _pool_body(xp_ref, w_ref, ps_ref, o_ref, pad_ref)


def _pool_body(xp_ref, w_ref, ps_ref, o_ref, pad_ref):
    seq = xp_ref.shape[0]
    zeros = jnp.zeros((POOL_PAD, POOL_WIDTH), F32)
    pad_ref[0:POOL_PAD, :] = zeros
    pad_ref[POOL_PAD + seq:POOL_PAD + seq + POOL_PAD, :] = zeros
    pad_ref[POOL_PAD:POOL_PAD + seq, :] = xp_ref[...]
    pos = lax.broadcasted_iota(jnp.int32, (seq, 1), 0)
    for g, win in enumerate(POOL_WINDOWS):
        half = win // 2
        c0 = g * POOL_GROUP_DIM
        acc = pad_ref[POOL_PAD - half:POOL_PAD - half + seq, c0:c0 + POOL_GROUP_DIM]
        for d in range(-half + 1, half):
            acc = acc + pad_ref[POOL_PAD + d:POOL_PAD + d + seq, c0:c0 + POOL_GROUP_DIM]
        count = (jnp.minimum(pos + half, seq) - jnp.maximum(pos - half, 0)).astype(F32)
        pooled = acc / count - xp_ref[:, c0:c0 + POOL_GROUP_DIM]
        mixed = jnp.dot(pooled.astype(BF16), w_ref[g], preferred_element_type=F32)
        o_ref[:, c0:c0 + POOL_GROUP_DIM] = (mixed * ps_ref[:, c0:c0 + POOL_GROUP_DIM]).astype(BF16)


def _pool_norms(h, seq, w_pool, pool_scale, g_q, g_kv, name):
    t = h.shape[0]
    nr = min(seq, NORM_ROWS)
    n_sub = seq // nr
    kr_w = 256
    cq_blk = POOL_WIDTH // Q_LORA_RANK
    ckv_blk = (POOL_WIDTH + Q_LORA_RANK) // KV_LORA_RANK
    kr_blk = (POOL_WIDTH + Q_LORA_RANK + KV_LORA_RANK) // kr_w
    return pl.pallas_call(
        _pool_norms_kernel,
        grid=(t // seq, n_sub),
        in_specs=[
            pl.BlockSpec((seq, POOL_WIDTH), lambda i, s: (i, 0)),
            pl.BlockSpec(w_pool.shape, lambda i, s: (0, 0, 0)),
            pl.BlockSpec((1, POOL_WIDTH), lambda i, s: (0, 0)),
            pl.BlockSpec((nr, Q_LORA_RANK), lambda i, s: (i * n_sub + s, cq_blk)),
            pl.BlockSpec((nr, KV_LORA_RANK), lambda i, s: (i * n_sub + s, ckv_blk)),
            pl.BlockSpec((nr, kr_w), lambda i, s: (i * n_sub + s, kr_blk)),
            pl.BlockSpec((1, Q_LORA_RANK), lambda i, s: (0, 0)),
            pl.BlockSpec((1, KV_LORA_RANK), lambda i, s: (0, 0)),
        ],
        out_specs=[
            pl.BlockSpec((seq, POOL_WIDTH), lambda i, s: (i, 0)),
            pl.BlockSpec((nr, Q_LORA_RANK), lambda i, s: (i * n_sub + s, 0)),
            pl.BlockSpec((nr, KV_LORA_RANK), lambda i, s: (i * n_sub + s, 0)),
            pl.BlockSpec((nr, QK_ROPE_DIM), lambda i, s: (i * n_sub + s, 0)),
        ],
        out_shape=[
            jax.ShapeDtypeStruct((t, POOL_WIDTH), BF16),
            jax.ShapeDtypeStruct((t, Q_LORA_RANK), BF16),
            jax.ShapeDtypeStruct((t, KV_LORA_RANK), F32),
            jax.ShapeDtypeStruct((t, QK_ROPE_DIM), F32),
        ],
        scratch_shapes=[pltpu.VMEM((seq + 2 * POOL_PAD, POOL_WIDTH), F32)],
        compiler_params=_cparams(("arbitrary", "arbitrary"), 56),
        name=name,
    )(h, w_pool, pool_scale, h, h, h, g_q, g_kv)


def _attn_kernel(*refs, la, lb):
    if lb:
        (qin_ref, mq_ref, ckva_ref, kra_ref, ck_ref, sk_ref, ckvb_ref, krb_ref,
         wuq_ref, wukv_ref, o_ref, kv_ref, krd_ref) = refs
    else:
        qin_ref, mq_ref, ckva_ref, kra_ref, wuq_ref, wukv_ref, o_ref, kv_ref, krd_ref = refs

    @pl.when(pl.program_id(1) == 0)
    def _():
        wukv = wukv_ref[...]
        for r0 in range(0, la, KV_ROWS):
            kv_ref[r0:r0 + KV_ROWS, :] = jnp.dot(
                ckva_ref[r0:r0 + KV_ROWS, :].astype(BF16), wukv, preferred_element_type=F32).astype(BF16)
        kr = kra_ref[...]
        if lb:
            kr2 = jnp.concatenate([kr, kr], axis=1)
            kr2 = kr2 * ck_ref[...] + pltpu.roll(kr2, QK_ROPE_DIM // 2, axis=1) * sk_ref[...]
            krd_ref[0:la, :] = kr2.astype(BF16)
            kv_ref[la:la + lb, :] = jnp.dot(
                ckvb_ref[0, 0].astype(BF16), wukv, preferred_element_type=F32).astype(BF16)
            krb = krb_ref[0, 0]
            krd_ref[la:la + lb, :] = jnp.concatenate([krb, krb], axis=1).astype(BF16)
        else:
            krd_ref[...] = jnp.concatenate([kr, jnp.zeros_like(kr)], axis=1).astype(BF16)

    q = jnp.dot(qin_ref[...], wuq_ref[...], preferred_element_type=F32)
    mq = mq_ref[...]
    krd = krd_ref[...]
    for h in range(N_HEADS):
        c0 = h * HEAD_SLOT
        qh = (q[:, c0:c0 + HEAD_SLOT] * mq).astype(BF16)
        kh = jnp.concatenate([kv_ref[:, c0:c0 + QK_NOPE_DIM], krd], axis=1)
        s = lax.dot_general(qh, kh, (((1,), (1,)), ((), ())), preferred_element_type=F32)
        e = jnp.exp(s - jnp.max(s, axis=-1, keepdims=True))
        denom = jnp.sum(e, axis=-1, keepdims=True)
        o = jnp.dot(e.astype(BF16), kv_ref[:, c0 + QK_NOPE_DIM:c0 + HEAD_SLOT], preferred_element_type=F32)
        o_ref[:, h * V_HEAD_DIM:(h + 1) * V_HEAD_DIM] = (o / denom).astype(BF16)


def _attention(q_in, mq, ckv, krope, w_uq, w_ukv, n_batch, la, name, rope=None, cache=None):
    nq = la // ATT_TQ
    lb = 0 if cache is None else cache[0].shape[2]
    mq_rows = mq.shape[0]
    in_specs = [
        pl.BlockSpec((ATT_TQ, Q_LORA_RANK), lambda b, qi: (b * nq + qi, 0)),
        pl.BlockSpec((ATT_TQ, HEAD_SLOT), (lambda b, qi: (qi, 0)) if mq_rows > ATT_TQ else (lambda b, qi: (0, 0))),
        pl.BlockSpec((la, KV_LORA_RANK), lambda b, qi: (b, 0)),
        pl.BlockSpec((la, QK_ROPE_DIM), lambda b, qi: (b, 0)),
    ]
    args = [q_in, mq, ckv, krope]
    if lb:
        ck, sk = rope
        cache_ckv, cache_krope = cache
        in_specs += [
            pl.BlockSpec((la, LANES), lambda b, qi: (0, 0)),
            pl.BlockSpec((la, LANES), lambda b, qi: (0, 0)),
            pl.BlockSpec((1, 1, lb, KV_LORA_RANK), lambda b, qi: (b, 0, 0, 0)),
            pl.BlockSpec((1, 1, lb, QK_ROPE_DIM), lambda b, qi: (b, 0, 0, 0)),
        ]
        args += [ck, sk, cache_ckv, cache_krope]
    in_specs += [
        pl.BlockSpec(w_uq.shape, lambda b, qi: (0, 0), pipeline_mode=pl.Buffered(1)),
        pl.BlockSpec(w_ukv.shape, lambda b, qi: (0, 0), pipeline_mode=pl.Buffered(1)),
    ]
    args += [w_uq, w_ukv]
    return pl.pallas_call(
        functools.partial(_attn_kernel, la=la, lb=lb),
        grid=(n_batch, nq),
        in_specs=in_specs,
        out_specs=pl.BlockSpec((ATT_TQ, N_HEADS * V_HEAD_DIM), lambda b, qi: (b * nq + qi, 0)),
        out_shape=jax.ShapeDtypeStruct((n_batch * la, N_HEADS * V_HEAD_DIM), BF16),
        scratch_shapes=[
            pltpu.VMEM((la + lb, N_HEADS * HEAD_SLOT), BF16),
            pltpu.VMEM((la + lb, LANES), BF16),
        ],
        compiler_params=_cparams(("arbitrary", "arbitrary"), 56),
        name=name,
    )(*args)


def _outproj_kernel(a_ref, p_ref, wa_ref, wp_ref, xp_ref, modp_ref, gam_ref, bet_ref, o_ref,
                    acc, acc_done, *, k0, nj, n_tiles):
    r = pl.program_id(0)
    j = pl.program_id(1)
    tn = wa_ref.shape[1]

    @pl.when(_and(r == 0, j == 0))
    def _():
        acc_done[...] = jnp.zeros(acc_done.shape, F32)

    def pieces():
        for s0 in range(0, OUT_PIECE, LN_ROWS):
            rows = pl.ds(pl.multiple_of(j * OUT_PIECE + s0, LN_ROWS), LN_ROWS)
            parts = []
            for jj in range(nj):
                cols = slice(jj * tn, (jj + 1) * tn)
                gate = modp_ref[0, k0 + 2:k0 + 3, cols]
                parts.append(DEEPNORM_ALPHA * xp_ref[s0:s0 + LN_ROWS, cols] + gate * acc_done[jj, rows, :])
            mu = sum(jnp.sum(p, axis=-1, keepdims=True) for p in parts) * (1.0 / D_MODEL)
            cen = [p - mu for p in parts]
            var = sum(jnp.sum(c * c, axis=-1, keepdims=True) for c in cen) * (1.0 / D_MODEL)
            rstd = lax.rsqrt(var + LN_EPS)
            for jj in range(nj):
                cols = slice(jj * tn, (jj + 1) * tn)
                o_ref[s0:s0 + LN_ROWS, cols] = cen[jj] * rstd * gam_ref[:, cols] + bet_ref[:, cols]

    def matmul():
        acc[j] = (jnp.dot(a_ref[...], wa_ref[...], preferred_element_type=F32)
                  + jnp.dot(p_ref[...], wp_ref[...], preferred_element_type=F32))

    def hand_over():
        acc_done[...] = acc[...]

    @pl.when(_and(r < n_tiles, j < nj - 1))
    def _():
        pieces()
        matmul()

    @pl.when(_and(r < n_tiles, j == nj - 1))
    def _():
        pieces()
        matmul()
        hand_over()

    @pl.when(r == n_tiles)
    def _():
        pieces()


def _outproj(attn, pool, w_out, x, mods, group_of_tile, k0, gam, bet, name):
    t = x.shape[0]
    n_tiles = t // OUT_TM
    nj = D_MODEL // OUT_TN
    assert OUT_TM // OUT_PIECE == nj
    half = attn.shape[1]

    def t_cur(r):
        return jnp.minimum(r, n_tiles - 1)

    def t_prev(r):
        return jnp.maximum(r - 1, 0)

    def w_blk(r, j):
        return jnp.where(r == n_tiles, nj - 1, j)

    return pl.pallas_call(
        functools.partial(_outproj_kernel, k0=k0, nj=nj, n_tiles=n_tiles),
        grid=(n_tiles + 1, nj),
        in_specs=[
            pl.BlockSpec((OUT_TM, half), lambda r, j: (t_cur(r), 0)),
            pl.BlockSpec((OUT_TM, half), lambda r, j: (t_cur(r), 0)),
            pl.BlockSpec((half, OUT_TN), lambda r, j: (0, w_blk(r, j))),
            pl.BlockSpec((half, OUT_TN), lambda r, j: (1, w_blk(r, j))),
            pl.BlockSpec((OUT_PIECE, D_MODEL), lambda r, j: (t_prev(r) * nj + j, 0)),
            pl.BlockSpec((1, N_MOD, D_MODEL), lambda r, j: (group_of_tile(t_prev(r), OUT_TM), 0, 0)),
            pl.BlockSpec((1, D_MODEL), lambda r, j: (0, 0)),
            pl.BlockSpec((1, D_MODEL), lambda r, j: (0, 0)),
        ],
        out_specs=pl.BlockSpec((OUT_PIECE, D_MODEL), lambda r, j: (jnp.where(r == 0, 0, t_prev(r) * nj + j), 0)),
        out_shape=jax.ShapeDtypeStruct((t, D_MODEL), F32),
        scratch_shapes=[
            pltpu.VMEM((nj, OUT_TM, OUT_TN), F32),
            pltpu.VMEM((nj, OUT_TM, OUT_TN), F32),
        ],
        compiler_params=_cparams(("arbitrary", "arbitrary"), 60),
        name=name,
    )(attn, pool, w_out, w_out, x, mods, gam, bet)


def _rope_tables(n_tokens):
    rows = n_tokens // GRID_W
    row_id = jnp.repeat(jnp.arange(rows, dtype=F32), GRID_W)
    col_id = jnp.tile(jnp.arange(GRID_W, dtype=F32), rows)
    n_freq = QK_ROPE_DIM // 4
    inv_freq = ROPE_THETA ** (-jnp.arange(n_freq, dtype=F32) / n_freq)
    ang = jnp.concatenate([row_id[:, None] * inv_freq, col_id[:, None] * inv_freq], -1)
    cos, sin = jnp.cos(ang), jnp.sin(ang)
    cos2 = jnp.concatenate([cos, cos], -1)
    sin2 = jnp.concatenate([-sin, sin], -1)
    ones = jnp.ones((n_tokens, QK_NOPE_DIM), F32)
    mq = ATTN_SCALE * jnp.concatenate([ones, cos2, sin2], -1)
    ck = jnp.concatenate([cos2, cos2], -1)
    sk = jnp.concatenate([sin2, sin2], -1)
    return mq, ck, sk


def _mixer(x1, mods, group_of_tile, seq, n_batch, wts, tag, rope=None, cache=None):
    (w_in, g_q, w_uq, g_kv, w_ukv, w_pool, pool_scale, w_out, gam, bet, mq) = wts
    h = _inproj(x1, mods, group_of_tile, 3, w_in, "inproj_" + tag)
    pool, q_in, ckv, krope = _pool_norms(h, seq, w_pool, pool_scale, g_q, g_kv, "pool_norms_" + tag)
    attn = _attention(q_in, mq, ckv, krope, w_uq, w_ukv, n_batch, seq, "attn_" + tag, rope=rope, cache=cache)
    x2 = _outproj(attn, pool, w_out, x1, mods, group_of_tile, 3, gam, bet, "outproj_" + tag)
    return x2, ckv, krope


def kernel(x_prompt, x_sample, cache_ckv, cache_krope, c, c_ctx, w_ada, b_ada, w_in, g_q, w_uq, g_kv, w_ukv,
           w_pool, pool_scale, w_out, w_ffn_gate, w_ffn_up, w_ffn_down, ln_gamma, ln_beta):
    n_ctx, l_ctx, _ = x_prompt.shape
    n_lat, l_lat, _ = x_sample.shape
    assert w_ada.shape[0] == DEPTH
    lyr = 0

    wg1 = w_ffn_gate[lyr, 0:1].astype(BF16)
    wu1 = w_ffn_up[lyr, 0:1].astype(BF16)
    wd1 = w_ffn_down[lyr, 0:1].astype(BF16)
    w_ukv_k = w_ukv[lyr].astype(BF16)
    w_pool_k = w_pool[lyr].astype(BF16)
    gam =[ln_gamma[lyr, k].reshape(1, D_MODEL) for k in range(3)]
    bet = [ln_beta[lyr, k].reshape(1, D_MODEL) for k in range(3)]
    g_q_k = g_q[lyr].reshape(1, Q_LORA_RANK)
    g_kv_k = g_kv[lyr].reshape(1, KV_LORA_RANK)
    ps = pool_scale[lyr].reshape(1, POOL_WIDTH)

    mq_lat, ck, sk = _rope_tables(l_lat)
    mq_ctx = jnp.broadcast_to(
        ATTN_SCALE * jnp.concatenate([jnp.ones((QK_NOPE_DIM + QK_ROPE_DIM,), F32), jnp.zeros((QK_ROPE_DIM,), F32)]),
        (ATT_TQ, HEAD_SLOT))

    mods, w_in_k, w_uq_k = _modulation(jnp.concatenate([c_ctx[None, :], c], axis=0), w_ada[lyr], b_ada[lyr],
                                       jnp.swapaxes(w_in[lyr], 0, 1), w_uq[lyr])

    def grp_ctx(i, tm):
        return 0

    def grp_lat(i, tm):
        return 1 + (i * tm) // l_lat

    x_ctx = x_prompt.reshape(n_ctx * l_ctx, D_MODEL)
    x_lat = x_sample.reshape(n_lat * l_lat, D_MODEL)

    x1_ctx, wg2, wu2, wd2 = _ffn(
        x_ctx, mods, grp_ctx, 0, 0, wg1, wu1, wd1, gam[0], bet[0], "ffn1_ctx",
        cast=((w_ffn_gate, (lyr, 1), CAST_ROWS_GATE_UP, 0),
              (w_ffn_up, (lyr, 1), CAST_ROWS_GATE_UP, 0),
              (w_ffn_down, (lyr, 1), CAST_ROWS_DOWN, 0)))
    x1_lat, w_out_k = _ffn(
        x_lat, mods, grp_lat, 0, 0, wg1, wu1, wd1, gam[0], bet[0], "ffn1_lat",
        cast=((w_out, (lyr,), CAST_ROWS_OUT, 0),))

    mix = (w_in_k, g_q_k, w_uq_k, g_kv_k, w_ukv_k, w_pool_k, ps, w_out_k, gam[1], bet[1])
    x2_ctx, ckv_ctx, krope_ctx = _mixer(x1_ctx, mods, grp_ctx, l_ctx, n_ctx, mix + (mq_ctx,), "ctx")
    x2_lat, _, _ = _mixer(x1_lat, mods, grp_lat, l_lat, n_lat, mix + (mq_lat,), "lat",
                          rope=(ck, sk), cache=(cache_ckv, cache_krope))

    wg2, wu2, wd2 = wg2[None], wu2[None], wd2[None]
    y_ctx, = _ffn(x2_ctx, mods, grp_ctx, 6, 0, wg2, wu2, wd2, gam[2], bet[2], "ffn2_ctx")
    y_lat, = _ffn(x2_lat, mods, grp_lat, 6, 0, wg2, wu2, wd2, gam[2], bet[2], "ffn2_lat")

    return (y_ctx.reshape(n_ctx, l_ctx, D_MODEL),
            y_lat.reshape(n_lat, l_lat, D_MODEL),
            ckv_ctx.reshape(n_ctx, DEPTH, l_ctx, KV_LORA_RANK),
            krope_ctx.reshape(n_ctx, DEPTH, l_ctx, QK_ROPE_DIM))
```
